```python
import jax, jax.numpy as jnp
from jax import lax
import numpy as np

D_MODEL = 1024
BATCH = 8
SEQ = 2048
DEPTH = 2
DEC_BATCH = 32
DEC_SEQ = 64
PAST_LEN = 1024

CHUNK = 64
N_LEFT_CHUNKS = 8
BAND_CHUNKS = N_LEFT_CHUNKS + 1
BAND_ROWS = N_LEFT_CHUNKS * CHUNK
N_A_LAYERS = DEPTH // 2
N_B_LAYERS = DEPTH - N_A_LAYERS
HEAD_DIM = 64
LRU_WIDTH = 3 * D_MODEL // 4
LRU_BLOCKS = LRU_WIDTH // HEAD_DIM
LRU_BLOCK = LRU_WIDTH // LRU_BLOCKS
CONV_WIDTH = 4
LRU_C = 8.0
ATT_WIDTH = 3 * D_MODEL // 4
N_ATT_HEADS = ATT_WIDTH // HEAD_DIM
N_MEM = 256
N_MEM_HEADS = 4
MEM_WIDTH = N_MEM_HEADS * HEAD_DIM
MIX_WIDTH = LRU_WIDTH + MEM_WIDTH
D_FF = -(-8 * D_MODEL // (3 * 256)) * 256
REL_CLIP = 256
RMS_EPS = 1e-6
NEG_INF = -1e30

kernel_name = 'yoco_rglru_chunkband_stream_step'


def rmsnorm(x, g):
    xf = x.astype(jnp.float32)
    y = xf * lax.rsqrt(jnp.mean(xf * xf, axis=-1, keepdims=True) + RMS_EPS)
    return (y * g.astype(jnp.float32)).astype(x.dtype)


def causal_conv(x, buf, w, b):
    T = x.shape[1]
    xp = jnp.concatenate([buf.astype(x.dtype), x], axis=1)
    y = b + xp[:, CONV_WIDTH - 1:CONV_WIDTH - 1 + T] * w[CONV_WIDTH - 1]
    for k in range(CONV_WIDTH - 1):
        y = y + xp[:, k:k + T] * w[k]
    return y, xp[:, T:]


def rglru(x, h0, w_a, b_a, w_i, b_i, lam, reset_first):
    B, T, _ = x.shape
    xb = x.reshape(B, T, LRU_BLOCKS, LRU_BLOCK)
    r = jax.nn.sigmoid(jnp.einsum('btni,nij->btnj', xb, w_a).reshape(B, T, LRU_WIDTH) + b_a)
    i = jax.nn.sigmoid(jnp.einsum('btni,nij->btnj', xb, w_i).reshape(B, T, LRU_WIDTH) + b_i)
    log_a = -LRU_C * r.astype(jnp.float32) * jax.nn.softplus(-lam.astype(jnp.float32))
    a = jnp.exp(log_a)
    mult = jnp.sqrt(-jnp.expm1(2.0 * log_a))
    if reset_first:
        mult = mult.at[:, 0].set(1.0)
    u = mult * (i * x).astype(jnp.float32)
    u = u.at[:, 0].add(a[:, 0] * h0.astype(jnp.float32))

    def combine(left, right):
        a_l, b_l = left
        a_r, b_r = right
        return a_l * a_r, a_r * b_l + b_r

    _, h = lax.associative_scan(combine, (a, u), axis=1)
    return h.astype(x.dtype), h[:, -1].astype(h0.dtype)


def rel_bias_lookup(rel, dist):
    idx = jnp.clip(dist, -REL_CLIP, REL_CLIP) + REL_CLIP
    return jnp.moveaxis(rel[idx], -1, 0).astype(jnp.float32)


def band_attend_prompt(q, k, v, rel):
    B, T, _ = q.shape
    NC = T // CHUNK
    BAND = BAND_CHUNKS * CHUNK
    qh = q.reshape(B, NC, CHUNK, N_ATT_HEADS, HEAD_DIM)
    pad = jnp.zeros((B, N_LEFT_CHUNKS * CHUNK, N_ATT_HEADS, HEAD_DIM), k.dtype)
    kc = jnp.concatenate([pad, k], axis=1).reshape(B, NC + N_LEFT_CHUNKS, CHUNK, N_ATT_HEADS, HEAD_DIM)
    vc = jnp.concatenate([pad.astype(v.dtype), v], axis=1).reshape(B, NC + N_LEFT_CHUNKS, CHUNK, N_ATT_HEADS, HEAD_DIM)
    kb = jnp.concatenate([kc[:, o:o + NC] for o in range(BAND_CHUNKS)], axis=2)
    vb = jnp.concatenate([vc[:, o:o + NC] for o in range(BAND_CHUNKS)], axis=2)
    s = jnp.einsum('bcqhd,bckhd->bhcqk', qh, kb).astype(jnp.float32) * (HEAD_DIM ** -0.5)
    qi = jnp.arange(CHUNK)[:, None]
    kj = jnp.arange(BAND)[None, :]
    bias = rel_bias_lookup(rel, N_LEFT_CHUNKS * CHUNK + qi - kj)
    kpos = (jnp.arange(NC)[:, None] - N_LEFT_CHUNKS) * CHUNK + jnp.arange(BAND)[None, :]
    s = s + bias[None, :, None]
    s = jnp.where((kpos >= 0)[None, None, :, None, :], s, NEG_INF)
    p = jax.nn.softmax(s, axis=-1).astype(v.dtype)
    o = jnp.einsum('bhcqk,bckhd->bcqhd', p, vb)
    return o.reshape(B, T, ATT_WIDTH)


def band_attend_sample(q, k_all, v_all, rel):
    B, T, _ = q.shape
    L = k_all.shape[1] - T
    qh = q.reshape(B, T, N_ATT_HEADS, HEAD_DIM)
    qpos = PAST_LEN + jnp.arange(T)
    kpos = PAST_LEN - L + jnp.arange(L + T)
    bias = rel_bias_lookup(rel, qpos[:, None] - kpos[None, :])
    s = jnp.einsum('bthd,bkhd->bhtk', qh, k_all).astype(jnp.float32) * (HEAD_DIM ** -0.5) + bias[None]
    p = jax.nn.softmax(s, axis=-1).astype(v_all.dtype)
    return jnp.einsum('bhtk,bkhd->bthd', p, v_all).reshape(B, T, ATT_WIDTH)


def mem_kv(mem, g, w):
    B = mem.shape[0]
    kv = rmsnorm(mem, g) @ w
    k = kv[..., :MEM_WIDTH].reshape(B, N_MEM, N_MEM_HEADS, HEAD_DIM)
    v = kv[..., MEM_WIDTH:].reshape(B, N_MEM, N_MEM_HEADS, HEAD_DIM)
    return k, v


def mem_attend(q, k, v):
    B, T, _ = q.shape
    qh = q.reshape(B, T, N_MEM_HEADS, HEAD_DIM)
    s = jnp.einsum('bthd,bmhd->bhtm', qh, k.astype(q.dtype)).astype(jnp.float32) * (HEAD_DIM ** -0.5)
    p = jax.nn.softmax(s, axis=-1).astype(q.dtype)
    return jnp.einsum('bhtm,bmhd->bthd', p, v.astype(q.dtype)).reshape(B, T, MEM_WIDTH)


def trunk(x, is_prompt, conv_buf, lru_h, past_k, past_v, mem_k, mem_v, p):
    new_conv, new_lru = [], []
    k_sh = None
    v_sh = None
    for l in range(DEPTH):
        h = rmsnorm(x, p['g_mix'][l])
        if l < N_A_LAYERS:
            proj = h @ p['w_in_a'][l]
            xr = proj[..., :LRU_WIDTH]
            gate = proj[..., LRU_WIDTH:2 * LRU_WIDTH]
            mq = proj[..., 2 * LRU_WIDTH:]
            xc, buf = causal_conv(xr, conv_buf[l], p['conv_w'][l], p['conv_b'][l])
            hs, h_last = rglru(xc, lru_h[l], p['w_rg_a'][l], p['b_rg_a'][l], p['w_rg_i'][l],
                               p['b_rg_i'][l], p['lru_lambda'][l], is_prompt)
            mix = hs * jax.nn.gelu(gate)
            new_conv.append(buf)
            new_lru.append(h_last)
        else:
            lb = l - N_A_LAYERS
            proj = h @ p['w_in_b'][lb]
            q = proj[..., :ATT_WIDTH]
            mq = proj[..., ATT_WIDTH:]
            if is_prompt:
                mix = band_attend_prompt(q, k_sh, v_sh, p['rel_bias'][lb])
            else:
                k_all = jnp.concatenate([past_k.astype(k_sh.dtype), k_sh], axis=1)
                v_all = jnp.concatenate([past_v.astype(v_sh.dtype), v_sh], axis=1)
                mix = band_attend_sample(q, k_all, v_all, p['rel_bias'][lb])
        mo = mem_attend(mq, mem_k[l], mem_v[l])
        x = x + jnp.concatenate([mix, mo], axis=-1) @ p['w_out'][l]
        hf = rmsnorm(x, p['g_ffn'][l])
        gu = hf @ p['w_ffn_gu'][l]
        x = x + (jax.nn.silu(gu[..., :D_FF]) * gu[..., D_FF:]) @ p['w_ffn_down'][l]
        if l == N_A_LAYERS - 1:
            B, T, _ = x.shape
            kv = rmsnorm(x, p['g_kv']) @ p['w_kv']
            k_sh = kv[..., :ATT_WIDTH].reshape(B, T, N_ATT_HEADS, HEAD_DIM)
            v_sh = kv[..., ATT_WIDTH:].reshape(B, T, N_ATT_HEADS, HEAD_DIM)
    y = rmsnorm(x, p['g_final'])
    return y, jnp.stack(new_conv), jnp.stack(new_lru), k_sh, v_sh


def setup_inputs(seed: int = 0) -> dict:
    key = jax.random.key(seed)
    ks = jax.random.split(key, 32)
    f32 = jnp.float32

    def nrm(k, shape, scale):
        return jax.random.normal(k, shape, f32) * scale

    band_len = min(BAND_ROWS, PAST_LEN)
    a0 = jax.random.uniform(ks[14], (N_A_LAYERS, LRU_WIDTH), f32, 0.9, 0.999)
    a = a0 ** (1.0 / LRU_C)
    lam = jnp.log(a) - jnp.log1p(-a)
    return {
        'x_prompt': nrm(ks[0], (BATCH, SEQ, D_MODEL), 1.0),
        'x_sample': nrm(ks[1], (DEC_BATCH, DEC_SEQ, D_MODEL), 1.0),
        'state_conv': nrm(ks[2], (N_A_LAYERS, DEC_BATCH, CONV_WIDTH - 1, LRU_WIDTH), 1.0),
        'state_lru': nrm(ks[3], (N_A_LAYERS, DEC_BATCH, LRU_WIDTH), 0.5),
        'cache_k': nrm(ks[4], (DEC_BATCH, band_len, N_ATT_HEADS, HEAD_DIM), 1.0),
        'cache_v': nrm(ks[5], (DEC_BATCH, band_len, N_ATT_HEADS, HEAD_DIM), 1.0),
        'cache_mem_k': nrm(ks[6], (DEPTH, DEC_BATCH, N_MEM, N_MEM_HEADS, HEAD_DIM), 1.0),
        'cache_mem_v': nrm(ks[7], (DEPTH, DEC_BATCH, N_MEM, N_MEM_HEADS, HEAD_DIM), 1.0),
        'mem_prompt': nrm(ks[8], (BATCH, N_MEM, D_MODEL), 1.0),
        'g_mix': 1.0 + nrm(ks[9], (DEPTH, D_MODEL), 0.05),
        'g_ffn': 1.0 + nrm(ks[10], (DEPTH, D_MODEL), 0.05),
        'g_final': 1.0 + nrm(ks[11], (D_MODEL,), 0.05),
        'w_in_a': nrm(ks[12], (N_A_LAYERS, D_MODEL, 2 * LRU_WIDTH + MEM_WIDTH), D_MODEL ** -0.5),
        'conv_w': nrm(ks[13], (N_A_LAYERS, CONV_WIDTH, LRU_WIDTH), CONV_WIDTH ** -0.5),
        'conv_b': nrm(ks[15], (N_A_LAYERS, LRU_WIDTH), 0.02),
        'w_rg_a': nrm(ks[16], (N_A_LAYERS, LRU_BLOCKS, LRU_BLOCK, LRU_BLOCK), LRU_BLOCK ** -0.5),
        'b_rg_a': nrm(ks[17], (N_A_LAYERS, LRU_WIDTH), 0.02),
        'w_rg_i': nrm(ks[18], (N_A_LAYERS, LRU_BLOCKS, LRU_BLOCK, LRU_BLOCK), LRU_BLOCK ** -0.5),
        'b_rg_i': nrm(ks[19], (N_A_LAYERS, LRU_WIDTH), 0.02),
        'lru_lambda': lam,
        'g_kv': 1.0 + nrm(ks[20], (D_MODEL,), 0.05),
        'w_kv': nrm(ks[21], (D_MODEL, 2 * ATT_WIDTH), D_MODEL ** -0.5),
        'w_in_b': nrm(ks[22], (N_B_LAYERS, D_MODEL, ATT_WIDTH + MEM_WIDTH), D_MODEL ** -0.5),
        'rel_bias': nrm(ks[23], (N_B_LAYERS, 2 * REL_CLIP + 1, N_ATT_HEADS), 0.5),
        'g_mem': 1.0 + nrm(ks[24], (DEPTH, D_MODEL), 0.05),
        'w_mem_kv': nrm(ks[25], (DEPTH, D_MODEL, 2 * MEM_WIDTH), D_MODEL ** -0.5),
        'w_out': nrm(ks[26], (DEPTH, MIX_WIDTH, D_MODEL), MIX_WIDTH ** -0.5),
        'w_ffn_gu': nrm(ks[27], (DEPTH, D_MODEL, 2 * D_FF), D_MODEL ** -0.5),
        'w_ffn_down': nrm(ks[28], (DEPTH, D_FF, D_MODEL), D_FF ** -0.5),
    }


def reference(x_prompt, x_sample, state_conv, state_lru, cache_k, cache_v, cache_mem_k, cache_mem_v,
              mem_prompt, g_mix, g_ffn, g_final, w_in_a, conv_w, conv_b, w_rg_a, b_rg_a, w_rg_i, b_rg_i,
              lru_lambda, g_kv, w_kv, w_in_b, rel_bias, g_mem, w_mem_kv, w_out, w_ffn_gu, w_ffn_down):
    p = {'g_mix': g_mix, 'g_ffn': g_ffn, 'g_final': g_final, 'w_in_a': w_in_a, 'conv_w': conv_w,
         'conv_b': conv_b, 'w_rg_a': w_rg_a, 'b_rg_a': b_rg_a, 'w_rg_i': w_rg_i, 'b_rg_i': b_rg_i,
         'lru_lambda': lru_lambda, 'g_kv': g_kv, 'w_kv': w_kv, 'w_in_b': w_in_b, 'rel_bias': rel_bias,
         'w_out': w_out, 'w_ffn_gu': w_ffn_gu, 'w_ffn_down': w_ffn_down}
    mk_list, mv_list = [], []
    for l in range(DEPTH):
        mk, mv = mem_kv(mem_prompt, g_mem[l], w_mem_kv[l])
        mk_list.append(mk)
        mv_list.append(mv)
    mem_k_p = jnp.stack(mk_list)
    mem_v_p = jnp.stack(mv_list)
    B = x_prompt.shape[0]
    conv0 = jnp.zeros((N_A_LAYERS, B, CONV_WIDTH - 1, LRU_WIDTH), x_prompt.dtype)
    lru0 = jnp.zeros((N_A_LAYERS, B, LRU_WIDTH), state_lru.dtype)
    y_prompt, conv_p, lru_p, k_p, v_p = trunk(x_prompt, True, conv0, lru0, None, None, mem_k_p, mem_v_p, p)
    y_sample, conv_s, lru_s, k_s, v_s = trunk(x_sample, False, state_conv, state_lru, cache_k, cache_v,
                                              cache_mem_k, cache_mem_v, p)
    nb = min(BAND_ROWS, x_prompt.shape[1])
    return (y_prompt, y_sample, conv_p, lru_p, k_p[:, -nb:], v_p[:, -nb:], mem_k_p, mem_v_p,
            conv_s, lru_s, k_s, v_s)
```

```python
import functools

import jax
import jax.numpy as jnp
from jax import lax
from jax.experimental import pallas as pl
from jax.experimental.pallas import tpu as pltpu

F32 = jnp.float32
BF16 = jnp.bfloat16

D_MODEL = 1024
HEAD_DIM = 64
LRU_WIDTH = 768
ATT_WIDTH = 768
N_ATT_HEADS = ATT_WIDTH // HEAD_DIM
MEM_WIDTH = 256
N_MEM_HEADS = MEM_WIDTH // HEAD_DIM
N_MEM = 256
D_FF = 2816
CONV_WIDTH = 4
CHUNK = 64
BAND_ROWS = 512
BAND_KEYS = BAND_ROWS + CHUNK
REL_CLIP = 256
LRU_C = 8.0
RMS_EPS = 1e-6
NEG_INF = -1e30
ATT_SCALE = HEAD_DIM ** -0.5

LANES = 128
N_LRU_LANE_BLOCKS = LRU_WIDTH // LANES
GATE_GROUP = 256
N_GATE_GROUPS = LRU_WIDTH // GATE_GROUP
CONV_PAD = 8
BIAS_TABLE_LEN = 1024
FF_CHUNKS = ((0, 1024), (1024, 2048), (2048, D_FF))
VMEM_LIMIT_BYTES = 58 * 1024 * 1024


def _rmsnorm(x, g):
    return x * lax.rsqrt(jnp.mean(x * x, axis=-1, keepdims=True) + RMS_EPS) * g


def _dot(a, b):
    return jnp.dot(a, b, preferred_element_type=F32)


def _dot_nt(a, b):
    return lax.dot_general(a, b, (((1,), (1,)), ((), ())), preferred_element_type=F32)


def _const_spec(shape):
    nd = len(shape)
    return pl.BlockSpec(shape, lambda *_: (0,) * nd, pipeline_mode=pl.Buffered(1))


def _params(n_axes):
    return pltpu.CompilerParams(dimension_semantics=("arbitrary",) * n_axes,
                                vmem_limit_bytes=VMEM_LIMIT_BYTES)


def _mem_kv_kernel(x_ref, g_ref, w_ref, k_ref, v_ref):
    h = _rmsnorm(x_ref[...], g_ref[...]).astype(BF16)
    kv = _dot(h, w_ref[...])
    k_ref[...] = kv[:, :MEM_WIDTH]
    v_ref[...] = kv[:, MEM_WIDTH:]


def _mem_kv(mem, g_mem, w_mem_kv):
    depth = g_mem.shape[0]
    n = mem.shape[0] * mem.shape[1]
    tm = 512
    x = mem.reshape(n, D_MODEL)
    out = jax.ShapeDtypeStruct((depth, n, MEM_WIDTH), F32)
    return pl.pallas_call(
        _mem_kv_kernel,
        grid=(depth, n // tm),
        in_specs=[
            pl.BlockSpec((tm, D_MODEL), lambda l, i: (i, 0)),
            pl.BlockSpec((None, 1, D_MODEL), lambda l, i: (l, 0, 0)),
            pl.BlockSpec((None, D_MODEL, 2 * MEM_WIDTH), lambda l, i: (l, 0, 0)),
        ],
        out_specs=[
            pl.BlockSpec((None, tm, MEM_WIDTH), lambda l, i: (l, i, 0)),
            pl.BlockSpec((None, tm, MEM_WIDTH), lambda l, i: (l, i, 0)),
        ],
        out_shape=[out, out],
        compiler_params=_params(2),
        name="mem_kv",
    )(x, g_mem.reshape(depth, 1, D_MODEL), w_mem_kv.astype(BF16))


def _softplus(y):
    return jnp.maximum(y, 0.0) + jnp.log1p(jnp.exp(-jnp.abs(y)))


def _gelu_tanh(x):
    c = 0.7978845608028654
    return x * (0.5 * (1.0 + jnp.tanh(c * (x + 0.044715 * (x * x * x)))))


def _mixer_a_kernel(x_ref, g_ref, win_ref, cw_ref, cb_ref, wa_ref, wi_ref, ba_ref, bi_ref,
                    lam_ref, conv0_ref, h0_ref,
                    mix_ref, mq_ref, convs_ref, lrus_ref,
                    xpad_s, a_s, u_s, h_s, carry_s, *, bb, tt, reset_first):
    t_idx = pl.program_id(1)
    rows = bb * tt
    hist_lo = CONV_PAD - (CONV_WIDTH - 1)

    @pl.when(t_idx == 0)
    def _init():
        xpad_s[:, hist_lo:CONV_PAD, :] = conv0_ref[...]
        carry_s[...] = h0_ref[...]

    x = x_ref[...].reshape(rows, D_MODEL)
    hn = _rmsnorm(x, g_ref[...]).astype(BF16)
    proj = _dot(hn, win_ref[...])
    gate = proj[:, LRU_WIDTH:2 * LRU_WIDTH]
    mq_ref[...] = proj[:, 2 * LRU_WIDTH:].reshape(bb, tt, MEM_WIDTH).astype(BF16)

    xpad_s[:, CONV_PAD:CONV_PAD + tt, :] = proj[:, :LRU_WIDTH].reshape(bb, tt, LRU_WIDTH)
    xc = cb_ref[...] + xpad_s[:, CONV_PAD:CONV_PAD + tt, :] * cw_ref[CONV_WIDTH - 1:CONV_WIDTH, :]
    for k in range(CONV_WIDTH - 1):
        xc = xc + xpad_s[:, hist_lo + k:hist_lo + k + tt, :] * cw_ref[k:k + 1, :]
    hist = xpad_s[:, tt + hist_lo:tt + CONV_PAD, :]
    convs_ref[...] = hist
    xpad_s[:, hist_lo:CONV_PAD, :] = hist

    xc2 = xc.reshape(rows, LRU_WIDTH)
    xcb = xc2.astype(BF16)
    r_parts, i_parts = [], []
    for gi in range(N_GATE_GROUPS):
        sl = slice(gi * GATE_GROUP, (gi + 1) * GATE_GROUP)
        r_parts.append(_dot(xcb[:, sl], wa_ref[gi]))
        i_parts.append(_dot(xcb[:, sl], wi_ref[gi]))
    r = jax.nn.sigmoid(jnp.concatenate(r_parts, axis=-1) + ba_ref[...])
    ig = jax.nn.sigmoid(jnp.concatenate(i_parts, axis=-1) + bi_ref[...])
    log_a = (-LRU_C * r) * _softplus(-lam_ref[...])
    a = jnp.exp(log_a)
    th = jnp.tanh(log_a)
    mult = jnp.sqrt(-2.0 * th / (1.0 - th))
    if reset_first:
        row_t = jnp.bitwise_and(lax.broadcasted_iota(jnp.int32, (rows, 1), 0), tt - 1)
        mult = jnp.where(jnp.logical_and(row_t == 0, t_idx == 0), 1.0, mult)
    u = mult * (ig * xc2)
    for c in range(N_LRU_LANE_BLOCKS):
        a_s[c] = a[:, c * LANES:(c + 1) * LANES]
        u_s[c] = u[:, c * LANES:(c + 1) * LANES]

    def step(t, hs):
        idx = pl.ds(t, bb, stride=tt)
        out = []
        for c in range(N_LRU_LANE_BLOCKS):
            h = a_s[c, idx, :] * hs[c] + u_s[c, idx, :]
            h_s[c, idx, :] = h
            out.append(h)
        return tuple(out)

    h0 = tuple(carry_s[:, c * LANES:(c + 1) * LANES] for c in range(N_LRU_LANE_BLOCKS))
    h_last = jnp.concatenate(lax.fori_loop(0, tt, step, h0, unroll=8), axis=-1)
    carry_s[...] = h_last
    lrus_ref[...] = h_last
    hs_all = jnp.concatenate([h_s[c] for c in range(N_LRU_LANE_BLOCKS)], axis=-1)
    mix_ref[...] = (hs_all * _gelu_tanh(gate)).reshape(bb, tt, LRU_WIDTH).astype(BF16)


def _mixer_a(x, g, w_in, conv_w, conv_b, wa_bd, wi_bd, b_a, b_i, lam, conv0, h0, *, bb, tt,
             reset_first):
    b, t, _ = x.shape
    assert b % bb == 0 and t % tt == 0 and tt & (tt - 1) == 0
    rows = bb * tt
    kern = functools.partial(_mixer_a_kernel, bb=bb, tt=tt, reset_first=reset_first)
    return pl.pallas_call(
        kern,
        grid=(b // bb, t // tt),
        in_specs=[
            pl.BlockSpec((bb, tt, D_MODEL), lambda i, j: (i, j, 0)),
            _const_spec((1, D_MODEL)),
            _const_spec((D_MODEL, 2 * LRU_WIDTH + MEM_WIDTH)),
            _const_spec((CONV_WIDTH, LRU_WIDTH)),
            _const_spec((1, LRU_WIDTH)),
            _const_spec((N_GATE_GROUPS, GATE_GROUP, GATE_GROUP)),
            _const_spec((N_GATE_GROUPS, GATE_GROUP, GATE_GROUP)),
            _const_spec((1, LRU_WIDTH)),
            _const_spec((1, LRU_WIDTH)),
            _const_spec((1, LRU_WIDTH)),
            pl.BlockSpec((bb, CONV_WIDTH - 1, LRU_WIDTH), lambda i, j: (i, 0, 0)),
            pl.BlockSpec((bb, LRU_WIDTH), lambda i, j: (i, 0)),
        ],
        out_specs=[
            pl.BlockSpec((bb, tt, LRU_WIDTH), lambda i, j: (i, j, 0)),
            pl.BlockSpec((bb, tt, MEM_WIDTH), lambda i, j: (i, j, 0)),
            pl.BlockSpec((bb, CONV_WIDTH - 1, LRU_WIDTH), lambda i, j: (i, 0, 0)),
            pl.BlockSpec((bb, LRU_WIDTH), lambda i, j: (i, 0)),
        ],
        out_shape=[
            jax.ShapeDtypeStruct((b, t, LRU_WIDTH), BF16),
            jax.ShapeDtypeStruct((b, t, MEM_WIDTH), BF16),
            jax.ShapeDtypeStruct((b, CONV_WIDTH - 1, LRU_WIDTH), F32),
            jax.ShapeDtypeStruct((b, LRU_WIDTH), F32),
        ],
        scratch_shapes=[
            pltpu.VMEM((bb, CONV_PAD + tt, LRU_WIDTH), F32),
            pltpu.VMEM((N_LRU_LANE_BLOCKS, rows, LANES), F32),
            pltpu.VMEM((N_LRU_LANE_BLOCKS, rows, LANES), F32),
            pltpu.VMEM((N_LRU_LANE_BLOCKS, rows, LANES), F32),
            pltpu.VMEM((bb, LRU_WIDTH), F32),
        ],
        compiler_params=_params(2),
        name="mixer_a",
    )(x, g, w_in, conv_w, conv_b, wa_bd, wi_bd, b_a, b_i, lam, conv0, h0)


def _post_kernel(x_ref, mix_ref, mq_ref, mk_ref, mv_ref, wo_ref, gf_ref, wg_ref, wu_ref, wd_ref,
                 gt_ref, *rest, bb, tq, with_kv):
    if with_kv:
        wk_ref, wv_ref, x2_ref, kf_ref, vf_ref, kb_ref, vb_ref, mo_s = rest
    else:
        y_ref, mo_s = rest
    rows = bb * tq
    mix_w = mix_ref.shape[-1]

    for b in range(bb):
        for h in range(N_MEM_HEADS):
            hs = slice(h * HEAD_DIM, (h + 1) * HEAD_DIM)
            q = mq_ref[b, :, hs] * ATT_SCALE
            k = mk_ref[b, :, hs].astype(BF16)
            v = mv_ref[b, :, hs].astype(BF16)
            s = _dot_nt(q, k)
            e = jnp.exp(s - jnp.max(s, axis=-1, keepdims=True))
            o = _dot(e.astype(BF16), v) / jnp.sum(e, axis=-1, keepdims=True)
            mo_s[b * tq:(b + 1) * tq, hs] = o.astype(BF16)

    x = x_ref[...].reshape(rows, D_MODEL)
    x1 = x + _dot(mix_ref[...].reshape(rows, mix_w), wo_ref[:mix_w, :]) + _dot(mo_s[...], wo_ref[mix_w:, :])
    hf = _rmsnorm(x1, gf_ref[...]).astype(BF16)
    acts = []
    for c0, c1 in FF_CHUNKS:
        gg = _dot(hf, wg_ref[:, c0:c1])
        uu = _dot(hf, wu_ref[:, c0:c1])
        acts.append(((gg * jax.nn.sigmoid(gg)) * uu).astype(BF16))
    x2 = x1 + _dot(jnp.concatenate(acts, axis=-1), wd_ref[...])

    if with_kv:
        x2_ref[...] = x2.reshape(bb, tq, D_MODEL)
        hk = _rmsnorm(x2, gt_ref[...]).astype(BF16)
        k = _dot(hk, wk_ref[...]).reshape(bb, tq, ATT_WIDTH)
        v = _dot(hk, wv_ref[...]).reshape(bb, tq, ATT_WIDTH)
        kf_ref[...] = k
        vf_ref[...] = v
        kb_ref[...] = k.astype(BF16)
        vb_ref[...] = v.astype(BF16)
    else:
        y_ref[...] = _rmsnorm(x2, gt_ref[...]).reshape(bb, tq, D_MODEL)


def _post(x, mix, mq, mem_k, mem_v, w_out, g_ffn, w_g, w_u, w_d, g_tail, w_k=None, w_v=None, *,
          bb, tq):
    b, t, _ = x.shape
    with_kv = w_k is not None
    assert b % bb == 0 and t % tq == 0
    mix_w = mix.shape[-1]
    tile = lambda w: pl.BlockSpec((bb, tq, w), lambda i, j: (i, j, 0))
    per_batch = lambda r, w: pl.BlockSpec((bb, r, w), lambda i, j: (i, 0, 0))
    in_specs = [
        tile(D_MODEL), tile(mix_w), tile(MEM_WIDTH),
        per_batch(N_MEM, MEM_WIDTH), per_batch(N_MEM, MEM_WIDTH),
        _const_spec((mix_w + MEM_WIDTH, D_MODEL)),
        _const_spec((1, D_MODEL)),
        _const_spec((D_MODEL, D_FF)), _const_spec((D_MODEL, D_FF)), _const_spec((D_FF, D_MODEL)),
        _const_spec((1, D_MODEL)),
    ]
    args = [x, mix, mq, mem_k, mem_v, w_out, g_ffn, w_g, w_u, w_d, g_tail]
    if with_kv:
        keep = min(BAND_ROWS, t)
        assert tq == keep
        in_specs += [_const_spec((D_MODEL, ATT_WIDTH)), _const_spec((D_MODEL, ATT_WIDTH))]
        args += [w_k, w_v]
        out_specs = [tile(D_MODEL), per_batch(keep, ATT_WIDTH), per_batch(keep, ATT_WIDTH),
                     tile(ATT_WIDTH), tile(ATT_WIDTH)]
        out_shape = [jax.ShapeDtypeStruct((b, t, D_MODEL), F32),
                     jax.ShapeDtypeStruct((b, keep, ATT_WIDTH), F32),
                     jax.ShapeDtypeStruct((b, keep, ATT_WIDTH), F32),
                     jax.ShapeDtypeStruct((b, t, ATT_WIDTH), BF16),
                     jax.ShapeDtypeStruct((b, t, ATT_WIDTH), BF16)]
    else:
        out_specs = [tile(D_MODEL)]
        out_shape = [jax.ShapeDtypeStruct((b, t, D_MODEL), F32)]
    kern = functools.partial(_post_kernel, bb=bb, tq=tq, with_kv=with_kv)
    return pl.pallas_call(
        kern,
        grid=(b // bb, t // tq),
        in_specs=in_specs,
        out_specs=out_specs,
        out_shape=out_shape,
        scratch_shapes=[pltpu.VMEM((bb * tq, MEM_WIDTH), BF16)],
        compiler_params=_params(2),
        name="post_kv" if with_kv else "post_final",
    )(*args)


def _mixer_b_kernel(x_ref, g_ref, win_ref, kb_ref, vb_ref, *rest, bb, tq, has_past):
    if has_past:
        pk_ref, pv_ref, tab_ref, mix_ref, mq_ref, kpad_s, vpad_s, bias_s, q_s = rest
    else:
        tab_ref, mix_ref, mq_ref, kpad_s, vpad_s, bias_s, q_s = rest
    bi = pl.program_id(0)
    j = pl.program_id(1)
    rows = bb * tq
    kw = BAND_ROWS + tq

    @pl.when(jnp.logical_and(bi == 0, j == 0))
    def _build_bias():
        for h in range(N_ATT_HEADS):
            row = jnp.broadcast_to(tab_ref[h:h + 1, :], (tq, BIAS_TABLE_LEN))
            bias_s[h] = pltpu.roll(row, 0, 1, stride=1, stride_axis=0)[:, :kw]

    @pl.when(j == 0)
    def _fill_kv():
        if has_past:
            kpad_s[:, :BAND_ROWS, :] = pk_ref[...].astype(BF16)
            vpad_s[:, :BAND_ROWS, :] = pv_ref[...].astype(BF16)
        else:
            kpad_s[:, :BAND_ROWS, :] = jnp.zeros((bb, BAND_ROWS, ATT_WIDTH), BF16)
            vpad_s[:, :BAND_ROWS, :] = jnp.zeros((bb, BAND_ROWS, ATT_WIDTH), BF16)
        kpad_s[:, BAND_ROWS:, :] = kb_ref[...]
        vpad_s[:, BAND_ROWS:, :] = vb_ref[...]

    x = x_ref[...].reshape(rows, D_MODEL)
    hn = _rmsnorm(x, g_ref[...]).astype(BF16)
    proj = _dot(hn, win_ref[...])
    mq_ref[...] = proj[:, ATT_WIDTH:].reshape(bb, tq, MEM_WIDTH).astype(BF16)
    q_s[...] = (proj[:, :ATT_WIDTH] * ATT_SCALE).reshape(bb, tq, ATT_WIDTH).astype(BF16)

    r_io = lax.broadcasted_iota(jnp.int32, (tq, kw), 0)
    i_io = lax.broadcasted_iota(jnp.int32, (tq, kw), 1)
    off = i_io - jnp.bitwise_and(r_io, -CHUNK)
    valid = jnp.logical_and(off >= 0, off < BAND_KEYS)
    if not has_past:
        valid = jnp.logical_and(valid, i_io >= BAND_ROWS - j * tq)
    start = pl.multiple_of(j * tq, tq)

    def one_batch(b, carry):
        for h in range(N_ATT_HEADS):
            hs = slice(h * HEAD_DIM, (h + 1) * HEAD_DIM)
            q = q_s[b, :, hs]
            k = kpad_s[b, pl.ds(start, kw), hs]
            v = vpad_s[b, pl.ds(start, kw), hs]
            s = jnp.where(valid, _dot_nt(q, k) + bias_s[h], NEG_INF)
            e = jnp.exp(s - jnp.max(s, axis=-1, keepdims=True))
            o = _dot(e.astype(BF16), v) / jnp.sum(e, axis=-1, keepdims=True)
            mix_ref[b, :, hs] = o.astype(BF16)
        return carry

    if bb == 1:
        one_batch(0, 0)
    else:
        lax.fori_loop(0, bb, one_batch, 0)


def _mixer_b(x, g, w_in, kb, vb, table, past_k=None, past_v=None, *, bb, tq):
    b, t, _ = x.shape
    has_past = past_k is not None
    assert b % bb == 0 and t % tq == 0 and tq % CHUNK == 0
    assert BAND_ROWS + 2 * tq - 1 <= BIAS_TABLE_LEN
    kw = BAND_ROWS + tq
    in_specs = [
        pl.BlockSpec((bb, tq, D_MODEL), lambda i, j: (i, j, 0)),
        _const_spec((1, D_MODEL)),
        _const_spec((D_MODEL, ATT_WIDTH + MEM_WIDTH)),
        pl.BlockSpec((bb, t, ATT_WIDTH), lambda i, j: (i, 0, 0)),
        pl.BlockSpec((bb, t, ATT_WIDTH), lambda i, j: (i, 0, 0)),
    ]
    args = [x, g, w_in, kb, vb]
    if has_past:
        in_specs += [pl.BlockSpec((bb, BAND_ROWS, ATT_WIDTH), lambda i, j: (i, 0, 0))] * 2
        args += [past_k, past_v]
    in_specs.append(_const_spec((N_ATT_HEADS, BIAS_TABLE_LEN)))
    args.append(table)
    kern = functools.partial(_mixer_b_kernel, bb=bb, tq=tq, has_past=has_past)
    return pl.pallas_call(
        kern,
        grid=(b // bb, t // tq),
        in_specs=in_specs,
        out_specs=[
            pl.BlockSpec((bb, tq, ATT_WIDTH), lambda i, j: (i, j, 0)),
            pl.BlockSpec((bb, tq, MEM_WIDTH), lambda i, j: (i, j, 0)),
        ],
        out_shape=[
            jax.ShapeDtypeStruct((b, t, ATT_WIDTH), BF16),
            jax.ShapeDtypeStruct((b, t, MEM_WIDTH), BF16),
        ],
        scratch_shapes=[
            pltpu.VMEM((bb, BAND_ROWS + t, ATT_WIDTH), BF16),
            pltpu.VMEM((bb, BAND_ROWS + t, ATT_WIDTH), BF16),
            pltpu.VMEM((N_ATT_HEADS, tq, kw), F32),
            pltpu.VMEM((bb, tq, ATT_WIDTH), BF16),
        ],
        compiler_params=_params(2),
        name="mixer_b_past" if has_past else "mixer_b",
    )(*args)


def _block_diag_groups(w):
    per = GATE_GROUP // HEAD_DIM
    w4 = w.reshape(N_GATE_GROUPS, per, HEAD_DIM, HEAD_DIM)
    eye = jnp.eye(per, dtype=w.dtype)
    bd = w4[:, :, :, None, :] * eye[None, :, None, :, None]
    return bd.reshape(N_GATE_GROUPS, GATE_GROUP, GATE_GROUP)


def _bias_table(rel):
    top = rel[-1:]
    head = jnp.broadcast_to(top, (BAND_ROWS - REL_CLIP, rel.shape[1]))
    tail = jnp.broadcast_to(top, (BIAS_TABLE_LEN - (BAND_ROWS - REL_CLIP) - rel.shape[0], rel.shape[1]))
    return jnp.concatenate([head, rel[::-1], tail], axis=0).T


def _trunk(x, is_prompt, conv0, h0, past_k, past_v, mem_k, mem_v, p, *, a_tile, post_tile, b_tile):
    b, t, _ = x.shape
    mix, mq, conv_s, lru_s = _mixer_a(
        x, p['g_mix'][0], p['w_in_a'], p['conv_w'], p['conv_b'], p['wa_bd'], p['wi_bd'], p['b_rg_a'],
        p['b_rg_i'], p['lam'], conv0, h0, bb=a_tile[0], tt=a_tile[1], reset_first=is_prompt)
    x, kf, vf, kb, vb = _post(
        x, mix, mq, mem_k[0], mem_v[0], p['w_out'][0], p['g_ffn'][0], p['w_g'][0], p['w_u'][0],
        p['w_d'][0], p['g_kv'], p['w_k'], p['w_v'], bb=post_tile[0], tq=post_tile[1])
    mix, mq = _mixer_b(x, p['g_mix'][1], p['w_in_b'], kb, vb, p['bias_table'], past_k, past_v,
                       bb=b_tile[0], tq=b_tile[1])
    (y,) = _post(
        x, mix, mq, mem_k[1], mem_v[1], p['w_out'][1], p['g_ffn'][1], p['w_g'][1], p['w_u'][1],
        p['w_d'][1], p['g_final'], bb=post_tile[0], tq=post_tile[1])
    keep = kf.shape[1]
    return (y, conv_s[None], lru_s[None], kf.reshape(b, keep, N_ATT_HEADS, HEAD_DIM),
            vf.reshape(b, keep, N_ATT_HEADS, HEAD_DIM))


def kernel(x_prompt, x_sample, state_conv, state_lru, cache_k, cache_v, cache_mem_k, cache_mem_v, mem_prompt, g_mix, g_ffn, g_final, w_in_a, conv_w, conv_b, w_rg_a, b_rg_a, w_rg_i, b_rg_i, lru_lambda, g_kv, w_kv, w_in_b, rel_bias, g_mem, w_mem_kv, w_out, w_ffn_gu, w_ffn_down):
    depth = g_mix.shape[0]
    row = lambda v: v.reshape(1, -1)
    p = {
        'g_mix': [row(g_mix[l]) for l in range(depth)],
        'g_ffn': [row(g_ffn[l]) for l in range(depth)],
        'g_final': row(g_final), 'g_kv': row(g_kv),
        'w_in_a': w_in_a[0].astype(BF16), 'conv_w': conv_w[0], 'conv_b': row(conv_b[0]),
        'wa_bd': _block_diag_groups(w_rg_a[0]).astype(BF16),
        'wi_bd': _block_diag_groups(w_rg_i[0]).astype(BF16),
        'b_rg_a': row(b_rg_a[0]), 'b_rg_i': row(b_rg_i[0]), 'lam': row(lru_lambda[0]),
        'w_k': w_kv[:, :ATT_WIDTH].astype(BF16), 'w_v': w_kv[:, ATT_WIDTH:].astype(BF16),
        'w_in_b': w_in_b[0].astype(BF16), 'bias_table': _bias_table(rel_bias[0]),
        'w_out': [w_out[l].astype(BF16) for l in range(depth)],
        'w_g': [w_ffn_gu[l, :, :D_FF].astype(BF16) for l in range(depth)],
        'w_u': [w_ffn_gu[l, :, D_FF:].astype(BF16) for l in range(depth)],
        'w_d': [w_ffn_down[l].astype(BF16) for l in range(depth)],
    }
    bp = x_prompt.shape[0]
    bs = x_sample.shape[0]

    mem_k_p, mem_v_p = _mem_kv(mem_prompt, g_mem, w_mem_kv)
    mem_k_p = mem_k_p.reshape(depth, bp, N_MEM, MEM_WIDTH)
    mem_v_p = mem_v_p.reshape(depth, bp, N_MEM, MEM_WIDTH)
    conv0 = jnp.zeros((bp, CONV_WIDTH - 1, LRU_WIDTH), F32)
    lru0 = jnp.zeros((bp, LRU_WIDTH), F32)
    y_p, conv_p, lru_p, k_p, v_p = _trunk(
        x_prompt, True, conv0, lru0, None, None, mem_k_p, mem_v_p, p,
        a_tile=(8, 64), post_tile=(1, 512), b_tile=(1, 256))

    mem_k_s = cache_mem_k.reshape(depth, bs, N_MEM, MEM_WIDTH)
    mem_v_s = cache_mem_v.reshape(depth, bs, N_MEM, MEM_WIDTH)
    y_s, conv_s, lru_s, k_s, v_s = _trunk(
        x_sample, False, state_conv[0], state_lru[0],
        cache_k.reshape(bs, BAND_ROWS, ATT_WIDTH), cache_v.reshape(bs, BAND_ROWS, ATT_WIDTH),
        mem_k_s, mem_v_s, p, a_tile=(8, 64), post_tile=(8, 64), b_tile=(4, 64))

    mem_shape = (depth, bp, N_MEM, N_MEM_HEADS, HEAD_DIM)
    return (y_p, y_s, conv_p, lru_p, k_p, v_p, mem_k_p.reshape(mem_shape), mem_v_p.reshape(mem_shape),
            conv_s, lru_s, k_s, v_s)
```

```python
import functools

import jax
import jax.numpy as jnp
from jax import lax
from jax.experimental import pallas as pl
from jax.experimental.pallas import tpu as pltpu

F32 = jnp.float32
BF16 = jnp.bfloat16

D_MODEL = 1024
HEAD_DIM = 64
LRU_WIDTH = 768
ATT_WIDTH = 768
N_ATT_HEADS = ATT_WIDTH // HEAD_DIM
MEM_WIDTH = 256
N_MEM_HEADS = MEM_WIDTH // HEAD_DIM
MIX_WIDTH = LRU_WIDTH + MEM_WIDTH
N_MEM = 256
D_FF = 2816
CONV_WIDTH = 4
CHUNK = 64
BAND_ROWS = 512
BAND_KEYS = BAND_ROWS + CHUNK
REL_CLIP = 256
LRU_C = 8.0
RMS_EPS = 1e-6
NEG_INF = -1e30
ATT_SCALE = HEAD_DIM ** -0.5

SUBLANES = 8
MXU_DIM = 256
GATE_GROUP = MXU_DIM
N_GATE_GROUPS = LRU_WIDTH // GATE_GROUP
BIAS_TABLE_LEN = 1024
FF_CHUNKS = ((0, 1024), (1024, 2048), (2048, D_FF))
VMEM_LIMIT_BYTES = 58 * 1024 * 1024


def _rmsnorm(x, g):
    return x * lax.rsqrt(jnp.mean(x * x, axis=-1, keepdims=True) + RMS_EPS) * g


def _dot(a, b):
    return jnp.dot(a, b, preferred_element_type=F32)


def _dot_nt(a, b):
    return lax.dot_general(a, b, (((1,), (1,)), ((), ())), preferred_element_type=F32)


def _const_spec(shape):
    nd = len(shape)
    return pl.BlockSpec(shape, lambda *_: (0,) * nd, pipeline_mode=pl.Buffered(1))


def _layer_spec(shape, layer):
    nd = len(shape)
    return pl.BlockSpec((None,) + tuple(shape), lambda *_: (layer,) + (0,) * nd,
                        pipeline_mode=pl.Buffered(1))


def _params(n_axes):
    return pltpu.CompilerParams(dimension_semantics=("arbitrary",) * n_axes,
                                vmem_limit_bytes=VMEM_LIMIT_BYTES)


def _mem_kv_kernel(x_ref, g_ref, w_ref, k_ref, v_ref):
    h = _rmsnorm(x_ref[...], g_ref[...]).astype(BF16)
    kv = _dot(h, w_ref[...])
    k_ref[...] = kv[:, :MEM_WIDTH]
    v_ref[...] = kv[:, MEM_WIDTH:]


def _mem_kv(mem, g_mem, w_mem_kv):
    depth = g_mem.shape[0]
    n = mem.shape[0] * mem.shape[1]
    tm = 512
    x = mem.reshape(n, D_MODEL)
    out = jax.ShapeDtypeStruct((depth, n, MEM_WIDTH), F32)
    return pl.pallas_call(
        _mem_kv_kernel,
        grid=(depth, n // tm),
        in_specs=[
            pl.BlockSpec((tm, D_MODEL), lambda l, i: (i, 0)),
            pl.BlockSpec((None, 1, D_MODEL), lambda l, i: (l, 0, 0)),
            pl.BlockSpec((None, D_MODEL, 2 * MEM_WIDTH), lambda l, i: (l, 0, 0)),
        ],
        out_specs=[
            pl.BlockSpec((None, tm, MEM_WIDTH), lambda l, i: (l, i, 0)),
            pl.BlockSpec((None, tm, MEM_WIDTH), lambda l, i: (l, i, 0)),
        ],
        out_shape=[out, out],
        compiler_params=_params(2),
        name="mem_kv",
    )(x, g_mem.reshape(depth, 1, D_MODEL), w_mem_kv)


def _softplus(y):
    return jnp.maximum(y, 0.0) + jnp.log1p(jnp.exp(-jnp.abs(y)))


def _gelu_tanh(x):
    c = 0.7978845608028654
    return x * (0.5 * (1.0 + jnp.tanh(c * (x + 0.044715 * (x * x * x)))))


def _mixer_a_kernel(x_ref, perm_ref, perm_t_ref, g_ref, win_ref, cw_ref, cb_ref, wa_ref, wi_ref,
                    ba_ref, bi_ref, lam_ref, conv0_ref, h0_ref,
                    mixq_ref, convs_ref, lrus_ref,
                    xpad_s, h_s, carry_s, *, bb, tt, reset_first):
    t_idx = pl.program_id(1)
    rows = bb * tt
    tsub = MXU_DIM // bb
    hist = (CONV_WIDTH - 1) * bb

    @pl.when(t_idx == 0)
    def _init():
        xpad_s[0:hist, :] = conv0_ref[...].reshape(hist, LRU_WIDTH)
        carry_s[...] = h0_ref[...]

    parts = []
    for s in range(tt // tsub):
        xs = x_ref[:, s * tsub:(s + 1) * tsub, :].reshape(MXU_DIM, D_MODEL)
        hn = _rmsnorm(xs, g_ref[...]).astype(BF16)
        parts.append(_dot(perm_ref[...], hn).astype(BF16))
    hn_tm = jnp.concatenate(parts, axis=0)
    proj = _dot(hn_tm, win_ref[...])
    gate = proj[:, LRU_WIDTH:2 * LRU_WIDTH]

    xpad_s[hist:hist + rows, :] = proj[:, :LRU_WIDTH]
    xc = cb_ref[...] + xpad_s[hist:hist + rows, :] * cw_ref[CONV_WIDTH - 1:CONV_WIDTH, :]
    for k in range(CONV_WIDTH - 1):
        xc = xc + xpad_s[k * bb:k * bb + rows, :] * cw_ref[k:k + 1, :]
    tail = xpad_s[rows:rows + hist, :]
    convs_ref[...] = tail.reshape(CONV_WIDTH - 1, bb, LRU_WIDTH)
    xpad_s[0:hist, :] = tail

    xcb = xc.astype(BF16)
    r_parts, i_parts = [], []
    for gi in range(N_GATE_GROUPS):
        sl = slice(gi * GATE_GROUP, (gi + 1) * GATE_GROUP)
        r_parts.append(_dot(xcb[:, sl], wa_ref[gi]))
        i_parts.append(_dot(xcb[:, sl], wi_ref[gi]))
    r = jax.nn.sigmoid(jnp.concatenate(r_parts, axis=-1) + ba_ref[...])
    ig = jax.nn.sigmoid(jnp.concatenate(i_parts, axis=-1) + bi_ref[...])
    log_a = (-LRU_C * r) * _softplus(-lam_ref[...])
    a = jnp.exp(log_a)
    th = jnp.tanh(log_a)
    mult = jnp.sqrt(-2.0 * th / (1.0 - th))
    if reset_first:
        row = lax.broadcasted_iota(jnp.int32, (rows, 1), 0)
        mult = jnp.where(jnp.logical_and(row < bb, t_idx == 0), 1.0, mult)
    u = mult * (ig * xc)

    h = carry_s[...]
    for t in range(tt):
        h = a[t * bb:(t + 1) * bb, :] * h + u[t * bb:(t + 1) * bb, :]
        h_s[t * bb:(t + 1) * bb, :] = h
    carry_s[...] = h
    lrus_ref[...] = h

    mix = (h_s[...] * _gelu_tanh(gate)).astype(BF16)
    mixq_tm = jnp.concatenate([mix, proj[:, 2 * LRU_WIDTH:].astype(BF16)], axis=-1)
    for s in range(tt // tsub):
        blk = _dot(perm_t_ref[...], mixq_tm[s * MXU_DIM:(s + 1) * MXU_DIM, :]).astype(BF16)
        mixq_ref[:, s * tsub:(s + 1) * tsub, :] = blk.reshape(bb, tsub, MIX_WIDTH)


def _mixer_a(x, g, w_in, conv_w, conv_b, wa_bd, wi_bd, b_a, b_i, lam, conv0_tm, h0, *, bb, tt,
             reset_first):
    b, t, _ = x.shape
    assert bb == SUBLANES and b % bb == 0 and t % tt == 0 and MXU_DIM % bb == 0
    tsub = MXU_DIM // bb
    assert tt % tsub == 0
    rows = bb * tt
    tm = jnp.arange(MXU_DIM)
    perm = (((tm % bb) * tsub + tm // bb)[:, None] == jnp.arange(MXU_DIM)[None, :]).astype(BF16)
    kern = functools.partial(_mixer_a_kernel, bb=bb, tt=tt, reset_first=reset_first)
    return pl.pallas_call(
        kern,
        grid=(b // bb, t // tt),
        in_specs=[
            pl.BlockSpec((bb, tt, D_MODEL), lambda i, j: (i, j, 0)),
            _const_spec((MXU_DIM, MXU_DIM)),
            _const_spec((MXU_DIM, MXU_DIM)),
            _const_spec((1, D_MODEL)),
            _const_spec((D_MODEL, 2 * LRU_WIDTH + MEM_WIDTH)),
            _const_spec((CONV_WIDTH, LRU_WIDTH)),
            _const_spec((1, LRU_WIDTH)),
            _const_spec((N_GATE_GROUPS, GATE_GROUP, GATE_GROUP)),
            _const_spec((N_GATE_GROUPS, GATE_GROUP, GATE_GROUP)),
            _const_spec((1, LRU_WIDTH)),
            _const_spec((1, LRU_WIDTH)),
            _const_spec((1, LRU_WIDTH)),
            pl.BlockSpec((CONV_WIDTH - 1, bb, LRU_WIDTH), lambda i, j: (0, i, 0)),
            pl.BlockSpec((bb, LRU_WIDTH), lambda i, j: (i, 0)),
        ],
        out_specs=[
            pl.BlockSpec((bb, tt, MIX_WIDTH), lambda i, j: (i, j, 0)),
            pl.BlockSpec((CONV_WIDTH - 1, bb, LRU_WIDTH), lambda i, j: (0, i, 0)),
            pl.BlockSpec((bb, LRU_WIDTH), lambda i, j: (i, 0)),
        ],
        out_shape=[
            jax.ShapeDtypeStruct((b, t, MIX_WIDTH), BF16),
            jax.ShapeDtypeStruct((CONV_WIDTH - 1, b, LRU_WIDTH), F32),
            jax.ShapeDtypeStruct((b, LRU_WIDTH), F32),
        ],
        scratch_shapes=[
            pltpu.VMEM(((CONV_WIDTH - 1) * bb + rows, LRU_WIDTH), F32),
            pltpu.VMEM((rows, LRU_WIDTH), F32),
            pltpu.VMEM((bb, LRU_WIDTH), F32),
        ],
        compiler_params=_params(2),
        name="mixer_a",
    )(x, perm, perm.T, g, w_in, conv_w, conv_b, wa_bd, wi_bd, b_a, b_i, lam, conv0_tm, h0)


def _post_kernel(x_ref, mixq_ref, mk_ref, mv_ref, wo_ref, gf_ref, wgu_ref, wd_ref, gt_ref, *rest,
                 bb, tq, with_kv):
    if with_kv:
        wkv_ref, x2_ref, kf_ref, vf_ref, kb_ref, vb_ref, cat_s = rest
    else:
        y_ref, cat_s = rest
    rows = bb * tq
    mix_w = MIX_WIDTH - MEM_WIDTH

    cat_s[:, :mix_w] = mixq_ref[:, :, :mix_w].reshape(rows, mix_w)
    for b in range(bb):
        for h in range(N_MEM_HEADS):
            hs = slice(h * HEAD_DIM, (h + 1) * HEAD_DIM)
            qs = slice(mix_w + h * HEAD_DIM, mix_w + (h + 1) * HEAD_DIM)
            q = mixq_ref[b, :, qs] * ATT_SCALE
            k = mk_ref[b, :, hs].astype(BF16)
            v = mv_ref[b, :, hs].astype(BF16)
            s = _dot_nt(q, k)
            e = jnp.exp(s - jnp.max(s, axis=-1, keepdims=True))
            o = _dot(e.astype(BF16), v) / jnp.sum(e, axis=-1, keepdims=True)
            cat_s[b * tq:(b + 1) * tq, qs] = o.astype(BF16)

    x = x_ref[...].reshape(rows, D_MODEL)
    x1 = x + _dot(cat_s[...], wo_ref[...])
    hf = _rmsnorm(x1, gf_ref[...]).astype(BF16)
    acts = []
    for c0, c1 in FF_CHUNKS:
        gg = _dot(hf, wgu_ref[:, c0:c1])
        uu = _dot(hf, wgu_ref[:, D_FF + c0:D_FF + c1])
        acts.append(((gg * jax.nn.sigmoid(gg)) * uu).astype(BF16))
    x2 = x1 + _dot(jnp.concatenate(acts, axis=-1), wd_ref[...])

    if with_kv:
        x2_ref[...] = x2.reshape(bb, tq, D_MODEL)
        hk = _rmsnorm(x2, gt_ref[...]).astype(BF16)
        kv = _dot(hk, wkv_ref[...])
        k = kv[:, :ATT_WIDTH].reshape(bb, tq, ATT_WIDTH)
        v = kv[:, ATT_WIDTH:].reshape(bb, tq, ATT_WIDTH)
        kf_ref[...] = k
        vf_ref[...] = v
        kb_ref[...] = k.astype(BF16)
        vb_ref[...] = v.astype(BF16)
    else:
        y_ref[...] = _rmsnorm(x2, gt_ref[...]).reshape(bb, tq, D_MODEL)


def _post(x, mixq, mem_k, mem_v, layer, w_out, g_ffn, w_gu, w_d, g_tail, w_kv=None, *, bb, tq):
    b, t, _ = x.shape
    with_kv = w_kv is not None
    assert b % bb == 0 and t % tq == 0
    tile = lambda w: pl.BlockSpec((bb, tq, w), lambda i, j: (i, j, 0))
    per_batch = lambda r, w: pl.BlockSpec((bb, r, w), lambda i, j: (i, 0, 0))
    mem_spec = pl.BlockSpec((None, bb, N_MEM, MEM_WIDTH), lambda i, j: (layer, i, 0, 0))
    in_specs = [
        tile(D_MODEL), tile(MIX_WIDTH), mem_spec, mem_spec,
        _layer_spec((MIX_WIDTH, D_MODEL), layer),
        _layer_spec((1, D_MODEL), layer),
        _layer_spec((D_MODEL, 2 * D_FF), layer),
        _layer_spec((D_FF, D_MODEL), layer),
        _const_spec((1, D_MODEL)),
    ]
    args = [x, mixq, mem_k, mem_v, w_out, g_ffn, w_gu, w_d, g_tail]
    if with_kv:
        keep = min(BAND_ROWS, t)
        assert tq == keep
        in_specs.append(_const_spec((D_MODEL, 2 * ATT_WIDTH)))
        args.append(w_kv)
        out_specs = [tile(D_MODEL), per_batch(keep, ATT_WIDTH), per_batch(keep, ATT_WIDTH),
                     tile(ATT_WIDTH), tile(ATT_WIDTH)]
        out_shape = [jax.ShapeDtypeStruct((b, t, D_MODEL), F32),
                     jax.ShapeDtypeStruct((b, keep, ATT_WIDTH), F32),
                     jax.ShapeDtypeStruct((b, keep, ATT_WIDTH), F32),
                     jax.ShapeDtypeStruct((b, t, ATT_WIDTH), BF16),
                     jax.ShapeDtypeStruct((b, t, ATT_WIDTH), BF16)]
    else:
        out_specs = [tile(D_MODEL)]
        out_shape = [jax.ShapeDtypeStruct((b, t, D_MODEL), F32)]
    kern = functools.partial(_post_kernel, bb=bb, tq=tq, with_kv=with_kv)
    return pl.pallas_call(
        kern,
        grid=(b // bb, t // tq),
        in_specs=in_specs,
        out_specs=out_specs,
        out_shape=out_shape,
        scratch_shapes=[pltpu.VMEM((bb * tq, MIX_WIDTH), BF16)],
        compiler_params=_params(2),
        name="post_kv" if with_kv else "post_final",
    )(*args)


def _mixer_b_kernel(x_ref, g_ref, win_ref, kb_ref, vb_ref, *rest, bb, tq, has_past):
    if has_past:
        pk_ref, pv_ref, tab_ref, mixq_ref, kpad_s, vpad_s, bias_s, q_s = rest
    else:
        tab_ref, mixq_ref, kpad_s, vpad_s, bias_s, q_s = rest
    bi = pl.program_id(0)
    j = pl.program_id(1)
    rows = bb * tq
    kw = BAND_ROWS + tq

    @pl.when(jnp.logical_and(bi == 0, j == 0))
    def _build_bias():
        for h in range(N_ATT_HEADS):
            row = jnp.broadcast_to(tab_ref[h:h + 1, :], (tq, BIAS_TABLE_LEN))
            bias_s[h] = pltpu.roll(row, 0, 1, stride=1, stride_axis=0)[:, :kw]

    @pl.when(j == 0)
    def _fill_kv():
        if has_past:
            kpad_s[:, :BAND_ROWS, :] = pk_ref[...].astype(BF16)
            vpad_s[:, :BAND_ROWS, :] = pv_ref[...].astype(BF16)
        else:
            kpad_s[:, :BAND_ROWS, :] = jnp.zeros((bb, BAND_ROWS, ATT_WIDTH), BF16)
            vpad_s[:, :BAND_ROWS, :] = jnp.zeros((bb, BAND_ROWS, ATT_WIDTH), BF16)
        kpad_s[:, BAND_ROWS:, :] = kb_ref[...]
        vpad_s[:, BAND_ROWS:, :] = vb_ref[...]

    x = x_ref[...].reshape(rows, D_MODEL)
    hn = _rmsnorm(x, g_ref[...]).astype(BF16)
    proj = _dot(hn, win_ref[...])
    mixq_ref[:, :, ATT_WIDTH:] = proj[:, ATT_WIDTH:].reshape(bb, tq, MEM_WIDTH).astype(BF16)
    q_s[...] = (proj[:, :ATT_WIDTH] * ATT_SCALE).reshape(bb, tq, ATT_WIDTH).astype(BF16)

    r_io = lax.broadcasted_iota(jnp.int32, (tq, kw), 0)
    i_io = lax.broadcasted_iota(jnp.int32, (tq, kw), 1)
    off = i_io - jnp.bitwise_and(r_io, -CHUNK)
    valid = jnp.logical_and(off >= 0, off < BAND_KEYS)
    if not has_past:
        valid = jnp.logical_and(valid, i_io >= BAND_ROWS - j * tq)
    start = pl.multiple_of(j * tq, tq)

    def one_batch(b, carry):
        for h in range(N_ATT_HEADS):
            hs = slice(h * HEAD_DIM, (h + 1) * HEAD_DIM)
            q = q_s[b, :, hs]
            k = kpad_s[b, pl.ds(start, kw), hs]
            v = vpad_s[b, pl.ds(start, kw), hs]
            s = jnp.where(valid, _dot_nt(q, k) + bias_s[h], NEG_INF)
            e = jnp.exp(s - jnp.max(s, axis=-1, keepdims=True))
            o = _dot(e.astype(BF16), v) / jnp.sum(e, axis=-1, keepdims=True)
            mixq_ref[b, :, hs] = o.astype(BF16)
        return carry

    if bb == 1:
        one_batch(0, 0)
    else:
        lax.fori_loop(0, bb, one_batch, 0)


def _mixer_b(x, g, w_in, kb, vb, table, past_k=None, past_v=None, *, bb, tq):
    b, t, _ = x.shape
    has_past = past_k is not None
    assert b % bb == 0 and t % tq == 0 and tq % CHUNK == 0
    assert BAND_ROWS + 2 * tq - 1 <= BIAS_TABLE_LEN
    kw = BAND_ROWS + tq
    in_specs = [
        pl.BlockSpec((bb, tq, D_MODEL), lambda i, j: (i, j, 0)),
        _const_spec((1, D_MODEL)),
        _const_spec((D_MODEL, ATT_WIDTH + MEM_WIDTH)),
        pl.BlockSpec((bb, t, ATT_WIDTH), lambda i, j: (i, 0, 0)),
        pl.BlockSpec((bb, t, ATT_WIDTH), lambda i, j: (i, 0, 0)),
    ]
    args = [x, g, w_in, kb, vb]
    if has_past:
        in_specs += [pl.BlockSpec((bb, BAND_ROWS, ATT_WIDTH), lambda i, j: (i, 0, 0))] * 2
        args += [past_k, past_v]
    in_specs.append(_const_spec((N_ATT_HEADS, BIAS_TABLE_LEN)))
    args.append(table)
    kern = functools.partial(_mixer_b_kernel, bb=bb, tq=tq, has_past=has_past)
    return pl.pallas_call(
        kern,
        grid=(b // bb, t // tq),
        in_specs=in_specs,
        out_specs=pl.BlockSpec((bb, tq, MIX_WIDTH), lambda i, j: (i, j, 0)),
        out_shape=jax.ShapeDtypeStruct((b, t, MIX_WIDTH), BF16),
        scratch_shapes=[
            pltpu.VMEM((bb, BAND_ROWS + t, ATT_WIDTH), BF16),
            pltpu.VMEM((bb, BAND_ROWS + t, ATT_WIDTH), BF16),
            pltpu.VMEM((N_ATT_HEADS, tq, kw), F32),
            pltpu.VMEM((bb, tq, ATT_WIDTH), BF16),
        ],
        compiler_params=_params(2),
        name="mixer_b_past" if has_past else "mixer_b",
    )(*args)


def _block_diag_groups(w):
    per = GATE_GROUP // HEAD_DIM
    w4 = w.reshape(N_GATE_GROUPS, per, HEAD_DIM, HEAD_DIM)
    eye = jnp.eye(per, dtype=w.dtype)
    bd = w4[:, :, :, None, :] * eye[None, :, None, :, None]
    return bd.reshape(N_GATE_GROUPS, GATE_GROUP, GATE_GROUP)


def _bias_table(rel):
    top = rel[-1:]
    head = jnp.broadcast_to(top, (BAND_ROWS - REL_CLIP, rel.shape[1]))
    tail = jnp.broadcast_to(top, (BIAS_TABLE_LEN - (BAND_ROWS - REL_CLIP) - rel.shape[0], rel.shape[1]))
    return jnp.concatenate([head, rel[::-1], tail], axis=0).T


def _trunk(x, is_prompt, conv0, h0, past_k, past_v, mem_k, mem_v, p, *, a_tile, post_tile, b_tile):
    b, t, _ = x.shape
    mixq, conv_s, lru_s = _mixer_a(
        x, p['g_mix'][0], p['w_in_a'], p['conv_w'], p['conv_b'], p['wa_bd'], p['wi_bd'], p['b_rg_a'],
        p['b_rg_i'], p['lam'], jnp.swapaxes(conv0, 0, 1), h0, bb=a_tile[0], tt=a_tile[1],
        reset_first=is_prompt)
    x, kf, vf, kb, vb = _post(
        x, mixq, mem_k, mem_v, 0, p['w_out'], p['g_ffn'], p['w_gu'], p['w_d'], p['g_kv'], p['w_kv'],
        bb=post_tile[0], tq=post_tile[1])
    mixq = _mixer_b(x, p['g_mix'][1], p['w_in_b'], kb, vb, p['bias_table'], past_k, past_v,
                    bb=b_tile[0], tq=b_tile[1])
    (y,) = _post(
        x, mixq, mem_k, mem_v, 1, p['w_out'], p['g_ffn'], p['w_gu'], p['w_d'], p['g_final'],
        bb=post_tile[0], tq=post_tile[1])
    keep = kf.shape[1]
    return (y, jnp.swapaxes(conv_s, 0, 1)[None], lru_s[None],
            kf.reshape(b, keep, N_ATT_HEADS, HEAD_DIM), vf.reshape(b, keep, N_ATT_HEADS, HEAD_DIM))


def kernel(x_prompt, x_sample, state_conv, state_lru, cache_k, cache_v, cache_mem_k, cache_mem_v, mem_prompt, g_mix, g_ffn, g_final, w_in_a, conv_w, conv_b, w_rg_a, b_rg_a, w_rg_i, b_rg_i, lru_lambda, g_kv, w_kv, w_in_b, rel_bias, g_mem, w_mem_kv, w_out, w_ffn_gu, w_ffn_down):
    depth = g_mix.shape[0]
    row = lambda v: v.reshape(1, -1)
    p = {
        'g_mix': [row(g_mix[l]) for l in range(depth)],
        'g_ffn': g_ffn.reshape(depth, 1, D_MODEL),
        'g_final': row(g_final), 'g_kv': row(g_kv),
        'w_in_a': w_in_a[0].astype(BF16), 'conv_w': conv_w[0], 'conv_b': row(conv_b[0]),
        'wa_bd': _block_diag_groups(w_rg_a[0]).astype(BF16),
        'wi_bd': _block_diag_groups(w_rg_i[0]).astype(BF16),
        'b_rg_a': row(b_rg_a[0]), 'b_rg_i': row(b_rg_i[0]), 'lam': row(lru_lambda[0]),
        'w_kv': w_kv.astype(BF16),
        'w_in_b': w_in_b[0].astype(BF16), 'bias_table': _bias_table(rel_bias[0]),
        'w_out': w_out.astype(BF16), 'w_gu': w_ffn_gu.astype(BF16), 'w_d': w_ffn_down.astype(BF16),
    }
    bp = x_prompt.shape[0]
    bs = x_sample.shape[0]

    mem_k_p, mem_v_p = _mem_kv(mem_prompt, g_mem, w_mem_kv.astype(BF16))
    mem_k_p = mem_k_p.reshape(depth, bp, N_MEM, MEM_WIDTH)
    mem_v_p = mem_v_p.reshape(depth, bp, N_MEM, MEM_WIDTH)
    conv0 = jnp.zeros((bp, CONV_WIDTH - 1, LRU_WIDTH), F32)
    lru0 = jnp.zeros((bp, LRU_WIDTH), F32)
    y_p, conv_p, lru_p, k_p, v_p = _trunk(
        x_prompt, True, conv0, lru0, None, None, mem_k_p, mem_v_p, p,
        a_tile=(8, 64), post_tile=(1, 512), b_tile=(1, 256))

    mem_k_s = cache_mem_k.reshape(depth, bs, N_MEM, MEM_WIDTH)
    mem_v_s = cache_mem_v.reshape(depth, bs, N_MEM, MEM_WIDTH)
    y_s, conv_s, lru_s, k_s, v_s = _trunk(
        x_sample, False, state_conv[0], state_lru[0],
        cache_k.reshape(bs, BAND_ROWS, ATT_WIDTH), cache_v.reshape(bs, BAND_ROWS, ATT_WIDTH),
        mem_k_s, mem_v_s, p, a_tile=(8, 64), post_tile=(8, 64), b_tile=(4, 64))

    mem_shape = (depth, bp, N_MEM, N_MEM_HEADS, HEAD_DIM)
    return (y_p, y_s, conv_p, lru_p, k_p, v_p, mem_k_p.reshape(mem_shape), mem_v_p.reshape(mem_shape),
            conv_s, lru_s, k_s, v_s)
```

```python
import functools

import jax
import jax.numpy as jnp
from jax import lax
from jax.experimental import pallas as pl
from jax.experimental.pallas import tpu as pltpu

F32 = jnp.float32
BF16 = jnp.bfloat16

D_MODEL = 1024
HEAD_DIM = 64
LRU_WIDTH = 768
ATT_WIDTH = 768
N_ATT_HEADS = ATT_WIDTH // HEAD_DIM
MEM_WIDTH = 256
N_MEM_HEADS = MEM_WIDTH // HEAD_DIM
MIX_WIDTH = LRU_WIDTH + MEM_WIDTH
N_MEM = 256
D_FF = 2816
CONV_WIDTH = 4
CHUNK = 64
BAND_ROWS = 512
BAND_KEYS = BAND_ROWS + CHUNK
REL_CLIP = 256
LRU_C = 8.0
RMS_EPS = 1e-6
NEG_INF = -1e30
ATT_SCALE = HEAD_DIM ** -0.5

SUBLANES = 8
MXU_DIM = 256
HEADS_PER_GROUP = MXU_DIM // HEAD_DIM
N_ATT_GROUPS = ATT_WIDTH // MXU_DIM
GATE_GROUP = MXU_DIM
N_GATE_GROUPS = LRU_WIDTH // GATE_GROUP
BIAS_TABLE_LEN = 1024
FF_CHUNKS = ((0, 1024), (1024, 2048), (2048, D_FF))
VMEM_LIMIT_BYTES = 58 * 1024 * 1024


def _rmsnorm(x, g):
    return x * lax.rsqrt(jnp.mean(x * x, axis=-1, keepdims=True) + RMS_EPS) * g


def _dot(a, b):
    return jnp.dot(a, b, preferred_element_type=F32)


def _dot_nt(a, b):
    return lax.dot_general(a, b, (((1,), (1,)), ((), ())), preferred_element_type=F32)


def _const_spec(shape):
    nd = len(shape)
    return pl.BlockSpec(shape, lambda *_: (0,) * nd, pipeline_mode=pl.Buffered(1))


def _layer_spec(shape, layer):
    nd = len(shape)
    return pl.BlockSpec((None,) + tuple(shape), lambda *_: (layer,) + (0,) * nd,
                        pipeline_mode=pl.Buffered(1))


def _lane_head(tq):
    return lax.broadcasted_iota(jnp.int32, (tq, MXU_DIM), 1) // HEAD_DIM


def _stack_heads(q):
    lane_head = _lane_head(q.shape[0])
    return jnp.concatenate(
        [jnp.where(lane_head == h, q, 0.0) for h in range(HEADS_PER_GROUP)], axis=0).astype(BF16)


def _head_group_attention(q_stack, k, v, bias=None, key_ok=None):
    tq = q_stack.shape[0] // HEADS_PER_GROUP
    s = _dot_nt(q_stack, k)
    if bias is not None:
        s = s + bias
    if key_ok is not None:
        s = jnp.where(key_ok, s, NEG_INF)
    e = jnp.exp(s - jnp.max(s, axis=-1, keepdims=True))
    o = _dot(e.astype(BF16), v) / jnp.sum(e, axis=-1, keepdims=True)
    lane_head = _lane_head(tq)
    out = jnp.where(lane_head == 0, o[:tq], 0.0)
    for h in range(1, HEADS_PER_GROUP):
        out = out + jnp.where(lane_head == h, o[h * tq:(h + 1) * tq], 0.0)
    return out


def _params(n_axes):
    return pltpu.CompilerParams(dimension_semantics=("arbitrary",) * n_axes,
                                vmem_limit_bytes=VMEM_LIMIT_BYTES)


def _mem_kv_kernel(x_ref, g_ref, w_ref, k_ref, v_ref):
    h = _rmsnorm(x_ref[...], g_ref[...]).astype(BF16)
    kv = _dot(h, w_ref[...])
    k_ref[...] = kv[:, :MEM_WIDTH]
    v_ref[...] = kv[:, MEM_WIDTH:]


def _mem_kv(mem, g_mem, w_mem_kv):
    depth = g_mem.shape[0]
    n = mem.shape[0] * mem.shape[1]
    tm = 512
    x = mem.reshape(n, D_MODEL)
    out = jax.ShapeDtypeStruct((depth, n, MEM_WIDTH), F32)
    return pl.pallas_call(
        _mem_kv_kernel,
        grid=(depth, n // tm),
        in_specs=[
            pl.BlockSpec((tm, D_MODEL), lambda l, i: (i, 0)),
            pl.BlockSpec((None, 1, D_MODEL), lambda l, i: (l, 0, 0)),
            pl.BlockSpec((None, D_MODEL, 2 * MEM_WIDTH), lambda l, i: (l, 0, 0)),
        ],
        out_specs=[
            pl.BlockSpec((None, tm, MEM_WIDTH), lambda l, i: (l, i, 0)),
            pl.BlockSpec((None, tm, MEM_WIDTH), lambda l, i: (l, i, 0)),
        ],
        out_shape=[out, out],
        compiler_params=_params(2),
        name="mem_kv",
    )(x, g_mem.reshape(depth, 1, D_MODEL), w_mem_kv)


def _softplus(y):
    return jnp.maximum(y, 0.0) + jnp.log1p(jnp.exp(-jnp.abs(y)))


def _gelu_tanh(x):
    c = 0.7978845608028654
    return x * (0.5 * (1.0 + jnp.tanh(c * (x + 0.044715 * (x * x * x)))))


def _mixer_a_kernel(x_ref, perm_ref, perm_t_ref, g_ref, win_ref, cw_ref, cb_ref, wa_ref, wi_ref,
                    ba_ref, bi_ref, lam_ref, conv0_ref, h0_ref,
                    mixq_ref, convs_ref, lrus_ref,
                    xpad_s, h_s, carry_s, *, bb, tt, reset_first):
    t_idx = pl.program_id(1)
    rows = bb * tt
    tsub = MXU_DIM // bb
    hist = (CONV_WIDTH - 1) * bb

    @pl.when(t_idx == 0)
    def _init():
        xpad_s[0:hist, :] = conv0_ref[...].reshape(hist, LRU_WIDTH)
        carry_s[...] = h0_ref[...]

    parts = []
    for s in range(tt // tsub):
        xs = x_ref[:, s * tsub:(s + 1) * tsub, :].reshape(MXU_DIM, D_MODEL)
        hn = _rmsnorm(xs, g_ref[...]).astype(BF16)
        parts.append(_dot(perm_ref[...], hn).astype(BF16))
    hn_tm = jnp.concatenate(parts, axis=0)
    proj = _dot(hn_tm, win_ref[...])
    gate = proj[:, LRU_WIDTH:2 * LRU_WIDTH]

    xpad_s[hist:hist + rows, :] = proj[:, :LRU_WIDTH]
    xc = cb_ref[...] + xpad_s[hist:hist + rows, :] * cw_ref[CONV_WIDTH - 1:CONV_WIDTH, :]
    for k in range(CONV_WIDTH - 1):
        xc = xc + xpad_s[k * bb:k * bb + rows, :] * cw_ref[k:k + 1, :]
    tail = xpad_s[rows:rows + hist, :]
    convs_ref[...] = tail.reshape(CONV_WIDTH - 1, bb, LRU_WIDTH)
    xpad_s[0:hist, :] = tail

    xcb = xc.astype(BF16)
    r_parts, i_parts = [], []
    for gi in range(N_GATE_GROUPS):
        sl = slice(gi * GATE_GROUP, (gi + 1) * GATE_GROUP)
        r_parts.append(_dot(xcb[:, sl], wa_ref[gi]))
        i_parts.append(_dot(xcb[:, sl], wi_ref[gi]))
    r = jax.nn.sigmoid(jnp.concatenate(r_parts, axis=-1) + ba_ref[...])
    ig = jax.nn.sigmoid(jnp.concatenate(i_parts, axis=-1) + bi_ref[...])
    log_a = (-LRU_C * r) * _softplus(-lam_ref[...])
    a = jnp.exp(log_a)
    th = jnp.tanh(log_a)
    mult = jnp.sqrt(-2.0 * th / (1.0 - th))
    if reset_first:
        row = lax.broadcasted_iota(jnp.int32, (rows, 1), 0)
        mult = jnp.where(jnp.logical_and(row < bb, t_idx == 0), 1.0, mult)
    u = mult * (ig * xc)

    h = carry_s[...]
    for t in range(tt):
        h = a[t * bb:(t + 1) * bb, :] * h + u[t * bb:(t + 1) * bb, :]
        h_s[t * bb:(t + 1) * bb, :] = h
    carry_s[...] = h
    lrus_ref[...] = h

    mix = (h_s[...] * _gelu_tanh(gate)).astype(BF16)
    mixq_tm = jnp.concatenate([mix, proj[:, 2 * LRU_WIDTH:].astype(BF16)], axis=-1)
    for s in range(tt // tsub):
        blk = _dot(perm_t_ref[...], mixq_tm[s * MXU_DIM:(s + 1) * MXU_DIM, :]).astype(BF16)
        mixq_ref[:, s * tsub:(s + 1) * tsub, :] = blk.reshape(bb, tsub, MIX_WIDTH)


def _mixer_a(x, g, w_in, conv_w, conv_b, wa_bd, wi_bd, b_a, b_i, lam, conv0_tm, h0, *, bb, tt,
             reset_first):
    b, t, _ = x.shape
    assert bb == SUBLANES and b % bb == 0 and t % tt == 0 and MXU_DIM % bb == 0
    tsub = MXU_DIM // bb
    assert tt % tsub == 0
    rows = bb * tt
    tm = jnp.arange(MXU_DIM)
    perm = (((tm % bb) * tsub + tm // bb)[:, None] == jnp.arange(MXU_DIM)[None, :]).astype(BF16)
    kern = functools.partial(_mixer_a_kernel, bb=bb, tt=tt, reset_first=reset_first)
    return pl.pallas_call(
        kern,
        grid=(b // bb, t // tt),
        in_specs=[
            pl.BlockSpec((bb, tt, D_MODEL), lambda i, j: (i, j, 0)),
            _const_spec((MXU_DIM, MXU_DIM)),
            _const_spec((MXU_DIM, MXU_DIM)),
            _const_spec((1, D_MODEL)),
            _const_spec((D_MODEL, 2 * LRU_WIDTH + MEM_WIDTH)),
            _const_spec((CONV_WIDTH, LRU_WIDTH)),
            _const_spec((1, LRU_WIDTH)),
            _const_spec((N_GATE_GROUPS, GATE_GROUP, GATE_GROUP)),
            _const_spec((N_GATE_GROUPS, GATE_GROUP, GATE_GROUP)),
            _const_spec((1, LRU_WIDTH)),
            _const_spec((1, LRU_WIDTH)),
            _const_spec((1, LRU_WIDTH)),
            pl.BlockSpec((CONV_WIDTH - 1, bb, LRU_WIDTH), lambda i, j: (0, i, 0)),
            pl.BlockSpec((bb, LRU_WIDTH), lambda i, j: (i, 0)),
        ],
        out_specs=[
            pl.BlockSpec((bb, tt, MIX_WIDTH), lambda i, j: (i, j, 0)),
            pl.BlockSpec((CONV_WIDTH - 1, bb, LRU_WIDTH), lambda i, j: (0, i, 0)),
            pl.BlockSpec((bb, LRU_WIDTH), lambda i, j: (i, 0)),
        ],
        out_shape=[
            jax.ShapeDtypeStruct((b, t, MIX_WIDTH), BF16),
            jax.ShapeDtypeStruct((CONV_WIDTH - 1, b, LRU_WIDTH), F32),
            jax.ShapeDtypeStruct((b, LRU_WIDTH), F32),
        ],
        scratch_shapes=[
            pltpu.VMEM(((CONV_WIDTH - 1) * bb + rows, LRU_WIDTH), F32),
            pltpu.VMEM((rows, LRU_WIDTH), F32),
            pltpu.VMEM((bb, LRU_WIDTH), F32),
        ],
        compiler_params=_params(2),
        name="mixer_a",
    )(x, perm, perm.T, g, w_in, conv_w, conv_b, wa_bd, wi_bd, b_a, b_i, lam, conv0_tm, h0)


def _post_kernel(x_ref, mixq_ref, mk_ref, mv_ref, wo_ref, gf_ref, wgu_ref, wd_ref, gt_ref, *rest,
                 bb, tq, with_kv):
    if with_kv:
        wkv_ref, x2_ref, kf_ref, vf_ref, kb_ref, vb_ref, cat_s = rest
    else:
        y_ref, cat_s = rest
    rows = bb * tq
    mix_w = MIX_WIDTH - MEM_WIDTH

    cat_s[:, :mix_w] = mixq_ref[:, :, :mix_w].reshape(rows, mix_w)
    for b in range(bb):
        q = mixq_ref[b, :, mix_w:].astype(F32) * ATT_SCALE
        o = _head_group_attention(_stack_heads(q), mk_ref[b].astype(BF16), mv_ref[b].astype(BF16))
        cat_s[b * tq:(b + 1) * tq, mix_w:] = o.astype(BF16)

    x = x_ref[...].reshape(rows, D_MODEL)
    x1 = x + _dot(cat_s[...], wo_ref[...])
    hf = _rmsnorm(x1, gf_ref[...]).astype(BF16)
    acts = []
    for c0, c1 in FF_CHUNKS:
        gg = _dot(hf, wgu_ref[:, c0:c1])
        uu = _dot(hf, wgu_ref[:, D_FF + c0:D_FF + c1])
        acts.append(((gg * jax.nn.sigmoid(gg)) * uu).astype(BF16))
    x2 = x1 + _dot(jnp.concatenate(acts, axis=-1), wd_ref[...])

    if with_kv:
        x2_ref[...] = x2.reshape(bb, tq, D_MODEL)
        hk = _rmsnorm(x2, gt_ref[...]).astype(BF16)
        kv = _dot(hk, wkv_ref[...])
        k = kv[:, :ATT_WIDTH].reshape(bb, tq, ATT_WIDTH)
        v = kv[:, ATT_WIDTH:].reshape(bb, tq, ATT_WIDTH)
        kf_ref[...] = k
        vf_ref[...] = v
        kb_ref[...] = k.astype(BF16)
        vb_ref[...] = v.astype(BF16)
    else:
        y_ref[...] = _rmsnorm(x2, gt_ref[...]).reshape(bb, tq, D_MODEL)


def _post(x, mixq, mem_k, mem_v, layer, w_out, g_ffn, w_gu, w_d, g_tail, w_kv=None, *, bb, tq):
    b, t, _ = x.shape
    with_kv = w_kv is not None
    assert b % bb == 0 and t % tq == 0
    tile = lambda w: pl.BlockSpec((bb, tq, w), lambda i, j: (i, j, 0))
    per_batch = lambda r, w: pl.BlockSpec((bb, r, w), lambda i, j: (i, 0, 0))
    mem_spec = pl.BlockSpec((None, bb, N_MEM, MEM_WIDTH), lambda i, j: (layer, i, 0, 0))
    in_specs = [
        tile(D_MODEL), tile(MIX_WIDTH), mem_spec, mem_spec,
        _layer_spec((MIX_WIDTH, D_MODEL), layer),
        _layer_spec((1, D_MODEL), layer),
        _layer_spec((D_MODEL, 2 * D_FF), layer),
        _layer_spec((D_FF, D_MODEL), layer),
        _const_spec((1, D_MODEL)),
    ]
    args = [x, mixq, mem_k, mem_v, w_out, g_ffn, w_gu, w_d, g_tail]
    if with_kv:
        keep = min(BAND_ROWS, t)
        assert tq == keep
        in_specs.append(_const_spec((D_MODEL, 2 * ATT_WIDTH)))
        args.append(w_kv)
        out_specs = [tile(D_MODEL), per_batch(keep, ATT_WIDTH), per_batch(keep, ATT_WIDTH),
                     tile(ATT_WIDTH), tile(ATT_WIDTH)]
        out_shape = [jax.ShapeDtypeStruct((b, t, D_MODEL), F32),
                     jax.ShapeDtypeStruct((b, keep, ATT_WIDTH), F32),
                     jax.ShapeDtypeStruct((b, keep, ATT_WIDTH), F32),
                     jax.ShapeDtypeStruct((b, t, ATT_WIDTH), BF16),
                     jax.ShapeDtypeStruct((b, t, ATT_WIDTH), BF16)]
    else:
        out_specs = [tile(D_MODEL)]
        out_shape = [jax.ShapeDtypeStruct((b, t, D_MODEL), F32)]
    kern = functools.partial(_post_kernel, bb=bb, tq=tq, with_kv=with_kv)
    return pl.pallas_call(
        kern,
        grid=(b // bb, t // tq),
        in_specs=in_specs,
        out_specs=out_specs,
        out_shape=out_shape,
        scratch_shapes=[pltpu.VMEM((bb * tq, MIX_WIDTH), BF16)],
        compiler_params=_params(2),
        name="post_kv" if with_kv else "post_final",
    )(*args)


def _mixer_b_kernel(x_ref, g_ref, win_ref, kb_ref, vb_ref, *rest, bb, tq, has_past):
    if has_past:
        pk_ref, pv_ref, tab_ref, mixq_ref, kpad_s, vpad_s, bias_s, qst_s = rest
    else:
        tab_ref, mixq_ref, kpad_s, vpad_s, bias_s, qst_s = rest
    bi = pl.program_id(0)
    j = pl.program_id(1)
    rows = bb * tq
    kw = BAND_ROWS + tq
    group_rows = HEADS_PER_GROUP * tq

    @pl.when(jnp.logical_and(bi == 0, j == 0))
    def _build_bias():
        r_io = lax.broadcasted_iota(jnp.int32, (tq, kw), 0)
        i_io = lax.broadcasted_iota(jnp.int32, (tq, kw), 1)
        off = i_io - jnp.bitwise_and(r_io, -CHUNK)
        in_band = jnp.logical_and(off >= 0, off < BAND_KEYS)
        for h in range(N_ATT_HEADS):
            row = jnp.broadcast_to(tab_ref[h:h + 1, :], (tq, BIAS_TABLE_LEN))
            rolled = pltpu.roll(row, 0, 1, stride=1, stride_axis=0)[:, :kw]
            bias_s[h * tq:(h + 1) * tq, :] = jnp.where(in_band, rolled, NEG_INF)

    @pl.when(j == 0)
    def _fill_kv():
        if has_past:
            kpad_s[:, :BAND_ROWS, :] = pk_ref[...].astype(BF16)
            vpad_s[:, :BAND_ROWS, :] = pv_ref[...].astype(BF16)
        else:
            kpad_s[:, :BAND_ROWS, :] = jnp.zeros((bb, BAND_ROWS, ATT_WIDTH), BF16)
            vpad_s[:, :BAND_ROWS, :] = jnp.zeros((bb, BAND_ROWS, ATT_WIDTH), BF16)
        kpad_s[:, BAND_ROWS:, :] = kb_ref[...]
        vpad_s[:, BAND_ROWS:, :] = vb_ref[...]

    x = x_ref[...].reshape(rows, D_MODEL)
    hn = _rmsnorm(x, g_ref[...]).astype(BF16)
    proj = _dot(hn, win_ref[...])
    mixq_ref[:, :, ATT_WIDTH:] = proj[:, ATT_WIDTH:].reshape(bb, tq, MEM_WIDTH).astype(BF16)
    lane_head = _lane_head(rows)
    for g in range(N_ATT_GROUPS):
        qg = proj[:, g * MXU_DIM:(g + 1) * MXU_DIM] * ATT_SCALE
        for h in range(HEADS_PER_GROUP):
            blk = jnp.where(lane_head == h, qg, 0.0).astype(BF16)
            qst_s[:, g, h * tq:(h + 1) * tq, :] = blk.reshape(bb, tq, MXU_DIM)

    start = pl.multiple_of(j * tq, tq)

    def attend(mask_keys):
        key_ok = None
        if mask_keys:
            key_ok = lax.broadcasted_iota(jnp.int32, (1, kw), 1) >= BAND_ROWS - j * tq

        def one_batch(b, carry):
            for g in range(N_ATT_GROUPS):
                ls = slice(g * MXU_DIM, (g + 1) * MXU_DIM)
                out = _head_group_attention(
                    qst_s[b, g], kpad_s[b, pl.ds(start, kw), ls], vpad_s[b, pl.ds(start, kw), ls],
                    bias_s[g * group_rows:(g + 1) * group_rows, :], key_ok)
                mixq_ref[b, :, ls] = out.astype(BF16)
            return carry

        if bb == 1:
            one_batch(0, 0)
        else:
            lax.fori_loop(0, bb, one_batch, 0)

    if has_past:
        attend(False)
    else:
        n_masked_tiles = BAND_ROWS // tq
        pl.when(j < n_masked_tiles)(lambda: attend(True))
        pl.when(j >= n_masked_tiles)(lambda: attend(False))


def _mixer_b(x, g, w_in, kb, vb, table, past_k=None, past_v=None, *, bb, tq):
    b, t, _ = x.shape
    has_past = past_k is not None
    assert b % bb == 0 and t % tq == 0 and tq % CHUNK == 0
    assert BAND_ROWS + 2 * tq - 1 <= BIAS_TABLE_LEN
    kw = BAND_ROWS + tq
    in_specs = [
        pl.BlockSpec((bb, tq, D_MODEL), lambda i, j: (i, j, 0)),
        _const_spec((1, D_MODEL)),
        _const_spec((D_MODEL, ATT_WIDTH + MEM_WIDTH)),
        pl.BlockSpec((bb, t, ATT_WIDTH), lambda i, j: (i, 0, 0)),
        pl.BlockSpec((bb, t, ATT_WIDTH), lambda i, j: (i, 0, 0)),
    ]
    args = [x, g, w_in, kb, vb]
    if has_past:
        in_specs += [pl.BlockSpec((bb, BAND_ROWS, ATT_WIDTH), lambda i, j: (i, 0, 0))] * 2
        args += [past_k, past_v]
    in_specs.append(_const_spec((N_ATT_HEADS, BIAS_TABLE_LEN)))
    args.append(table)
    kern = functools.partial(_mixer_b_kernel, bb=bb, tq=tq, has_past=has_past)
    return pl.pallas_call(
        kern,
        grid=(b // bb, t // tq),
        in_specs=in_specs,
        out_specs=pl.BlockSpec((bb, tq, MIX_WIDTH), lambda i, j: (i, j, 0)),
        out_shape=jax.ShapeDtypeStruct((b, t, MIX_WIDTH), BF16),
        scratch_shapes=[
            pltpu.VMEM((bb, BAND_ROWS + t, ATT_WIDTH), BF16),
            pltpu.VMEM((bb, BAND_ROWS + t, ATT_WIDTH), BF16),
            pltpu.VMEM((N_ATT_HEADS * tq, kw), F32),
            pltpu.VMEM((bb, N_ATT_GROUPS, HEADS_PER_GROUP * tq, MXU_DIM), BF16),
        ],
        compiler_params=_params(2),
        name="mixer_b_past" if has_past else "mixer_b",
    )(*args)


def _block_diag_groups(w):
    per = GATE_GROUP // HEAD_DIM
    w4 = w.reshape(N_GATE_GROUPS, per, HEAD_DIM, HEAD_DIM)
    eye = jnp.eye(per, dtype=w.dtype)
    bd = w4[:, :, :, None, :] * eye[None, :, None, :, None]
    return bd.reshape(N_GATE_GROUPS, GATE_GROUP, GATE_GROUP)


def _bias_table(rel):
    top = rel[-1:]
    head = jnp.broadcast_to(top, (BAND_ROWS - REL_CLIP, rel.shape[1]))
    tail = jnp.broadcast_to(top, (BIAS_TABLE_LEN - (BAND_ROWS - REL_CLIP) - rel.shape[0], rel.shape[1]))
    return jnp.concatenate([head, rel[::-1], tail], axis=0).T


def _trunk(x, is_prompt, conv0, h0, past_k, past_v, mem_k, mem_v, p, *, a_tile, post_tile, b_tile):
    b, t, _ = x.shape
    mixq, conv_s, lru_s = _mixer_a(
        x, p['g_mix'][0], p['w_in_a'], p['conv_w'], p['conv_b'], p['wa_bd'], p['wi_bd'], p['b_rg_a'],
        p['b_rg_i'], p['lam'], jnp.swapaxes(conv0, 0, 1), h0, bb=a_tile[0], tt=a_tile[1],
        reset_first=is_prompt)
    x, kf, vf, kb, vb = _post(
        x, mixq, mem_k, mem_v, 0, p['w_out'], p['g_ffn'], p['w_gu'], p['w_d'], p['g_kv'], p['w_kv'],
        bb=post_tile[0], tq=post_tile[1])
    mixq = _mixer_b(x, p['g_mix'][1], p['w_in_b'], kb, vb, p['bias_table'], past_k, past_v,
                    bb=b_tile[0], tq=b_tile[1])
    (y,) = _post(
        x, mixq, mem_k, mem_v, 1, p['w_out'], p['g_ffn'], p['w_gu'], p['w_d'], p['g_final'],
        bb=post_tile[0], tq=post_tile[1])
    keep = kf.shape[1]
    return (y, jnp.swapaxes(conv_s, 0, 1)[None], lru_s[None],
            kf.reshape(b, keep, N_ATT_HEADS, HEAD_DIM), vf.reshape(b, keep, N_ATT_HEADS, HEAD_DIM))


def kernel(x_prompt, x_sample, state_conv, state_lru, cache_k, cache_v, cache_mem_k, cache_mem_v, mem_prompt, g_mix, g_ffn, g_final, w_in_a, conv_w, conv_b, w_rg_a, b_rg_a, w_rg_i, b_rg_i, lru_lambda, g_kv, w_kv, w_in_b, rel_bias, g_mem, w_mem_kv, w_out, w_ffn_gu, w_ffn_down):
    depth = g_mix.shape[0]
    row = lambda v: v.reshape(1, -1)
    p = {
        'g_mix': [row(g_mix[l]) for l in range(depth)],
        'g_ffn': g_ffn.reshape(depth, 1, D_MODEL),
        'g_final': row(g_final), 'g_kv': row(g_kv),
        'w_in_a': w_in_a[0].astype(BF16), 'conv_w': conv_w[0], 'conv_b': row(conv_b[0]),
        'wa_bd': _block_diag_groups(w_rg_a[0]).astype(BF16),
        'wi_bd': _block_diag_groups(w_rg_i[0]).astype(BF16),
        'b_rg_a': row(b_rg_a[0]), 'b_rg_i': row(b_rg_i[0]), 'lam': row(lru_lambda[0]),
        'w_kv': w_kv.astype(BF16),
        'w_in_b': w_in_b[0].astype(BF16), 'bias_table': _bias_table(rel_bias[0]),
        'w_out': w_out.astype(BF16), 'w_gu': w_ffn_gu.astype(BF16), 'w_d': w_ffn_down.astype(BF16),
    }
    bp = x_prompt.shape[0]
    bs = x_sample.shape[0]

    mem_k_p, mem_v_p = _mem_kv(mem_prompt, g_mem, w_mem_kv.astype(BF16))
    mem_k_p = mem_k_p.reshape(depth, bp, N_MEM, MEM_WIDTH)
    mem_v_p = mem_v_p.reshape(depth, bp, N_MEM, MEM_WIDTH)
    conv0 = jnp.zeros((bp, CONV_WIDTH - 1, LRU_WIDTH), F32)
    lru0 = jnp.zeros((bp, LRU_WIDTH), F32)
    y_p, conv_p, lru_p, k_p, v_p = _trunk(
        x_prompt, True, conv0, lru0, None, None, mem_k_p, mem_v_p, p,
        a_tile=(8, 64), post_tile=(1, 512), b_tile=(1, 256))

    mem_k_s = cache_mem_k.reshape(depth, bs, N_MEM, MEM_WIDTH)
    mem_v_s = cache_mem_v.reshape(depth, bs, N_MEM, MEM_WIDTH)
    y_s, conv_s, lru_s, k_s, v_s = _trunk(
        x_sample, False, state_conv[0], state_lru[0],
        cache_k.reshape(bs, BAND_ROWS, ATT_WIDTH), cache_v.reshape(bs, BAND_ROWS, ATT_WIDTH),
        mem_k_s, mem_v_s, p, a_tile=(8, 64), post_tile=(8, 64), b_tile=(4, 64))

    mem_shape = (depth, bp, N_MEM, N_MEM_HEADS, HEAD_DIM)
    return (y_p, y_s, conv_p, lru_p, k_p, v_p, mem_k_p.reshape(mem_shape), mem_v_p.reshape(mem_shape),
            conv_s, lru_s, k_s, v_s)
```

```python
import functools

import jax
import jax.numpy as jnp
from jax import lax
from jax.experimental import pallas as pl
from jax.experimental.pallas import tpu as pltpu

F32 = jnp.float32
BF16 = jnp.bfloat16

D_MODEL = 1024
HEAD_DIM = 64
LRU_WIDTH = 768
ATT_WIDTH = 768
N_ATT_HEADS = ATT_WIDTH // HEAD_DIM
MEM_WIDTH = 256
N_MEM_HEADS = MEM_WIDTH // HEAD_DIM
MIX_WIDTH = LRU_WIDTH + MEM_WIDTH
N_MEM = 256
D_FF = 2816
CONV_WIDTH = 4
CHUNK = 64
BAND_ROWS = 512
BAND_KEYS = BAND_ROWS + CHUNK
REL_CLIP = 256
LRU_C = 8.0
RMS_EPS = 1e-6
NEG_INF = -1e30
ATT_SCALE = HEAD_DIM ** -0.5
LOG2_E = 1.4426950408889634
Q_SCALE = ATT_SCALE * LOG2_E

SUBLANES = 8
MXU_DIM = 256
HEADS_PER_GROUP = MXU_DIM // HEAD_DIM
N_ATT_GROUPS = ATT_WIDTH // MXU_DIM
GATE_GROUP = MXU_DIM
N_GATE_GROUPS = LRU_WIDTH // GATE_GROUP
BIAS_TABLE_LEN = 1024
FF_CHUNKS = ((0, 1024), (1024, 2048), (2048, D_FF))
VMEM_LIMIT_BYTES = 58 * 1024 * 1024


def _rmsnorm(x, g):
    return x * lax.rsqrt(jnp.mean(x * x, axis=-1, keepdims=True) + RMS_EPS) * g


def _dot(a, b):
    return jnp.dot(a, b, preferred_element_type=F32)


def _dot_nt(a, b):
    return lax.dot_general(a, b, (((1,), (1,)), ((), ())), preferred_element_type=F32)


def _const_spec(shape):
    nd = len(shape)
    return pl.BlockSpec(shape, lambda *_: (0,) * nd, pipeline_mode=pl.Buffered(1))


def _layer_spec(shape, layer):
    nd = len(shape)
    return pl.BlockSpec((None,) + tuple(shape), lambda *_: (layer,) + (0,) * nd,
                        pipeline_mode=pl.Buffered(1))


def _lane_head(tq):
    return lax.broadcasted_iota(jnp.int32, (tq, MXU_DIM), 1) // HEAD_DIM


def _stack_heads(q):
    lane_head = _lane_head(q.shape[0])
    return jnp.concatenate(
        [jnp.where(lane_head == h, q, 0.0) for h in range(HEADS_PER_GROUP)], axis=0).astype(BF16)


def _group_scores(q_stack, k, bias=None, key_ok=None, *, k_is_transposed=False):
    s = _dot(q_stack, k) if k_is_transposed else _dot_nt(q_stack, k)
    if bias is not None:
        s = s + bias
    if key_ok is not None:
        s = jnp.where(key_ok, s, NEG_INF)
    return s


def _group_exp(s):
    e = jnp.exp2(s - jnp.max(s, axis=-1, keepdims=True))
    return e.astype(BF16), jnp.sum(e, axis=-1, keepdims=True)


def _group_values(e, row_sum, v, *, v_is_transposed=False):
    tq = e.shape[0] // HEADS_PER_GROUP
    o = (_dot_nt(e, v) if v_is_transposed else _dot(e, v)) / row_sum
    lane_head = _lane_head(tq)
    out = jnp.where(lane_head == 0, o[:tq], 0.0)
    for h in range(1, HEADS_PER_GROUP):
        out = out + jnp.where(lane_head == h, o[h * tq:(h + 1) * tq], 0.0)
    return out


def _params(n_axes):
    return pltpu.CompilerParams(dimension_semantics=("arbitrary",) * n_axes,
                                vmem_limit_bytes=VMEM_LIMIT_BYTES)


def _mem_kv_kernel(x_ref, g_ref, w_ref, kt_ref, vt_ref, *, nb):
    h = _rmsnorm(x_ref[...].reshape(nb * N_MEM, D_MODEL), g_ref[...]).astype(BF16)
    kv = _dot(h, w_ref[...])
    for b in range(nb):
        blk = kv[b * N_MEM:(b + 1) * N_MEM, :]
        kt_ref[b] = blk[:, :MEM_WIDTH].T
        vt_ref[b] = blk[:, MEM_WIDTH:].T


def _mem_kv(mem, g_mem, w_mem_kv):
    depth = g_mem.shape[0]
    b = mem.shape[0]
    nb = 2
    assert b % nb == 0
    out = jax.ShapeDtypeStruct((depth, b, MEM_WIDTH, N_MEM), F32)
    return pl.pallas_call(
        functools.partial(_mem_kv_kernel, nb=nb),
        grid=(depth, b // nb),
        in_specs=[
            pl.BlockSpec((nb, N_MEM, D_MODEL), lambda l, i: (i, 0, 0)),
            pl.BlockSpec((None, 1, D_MODEL), lambda l, i: (l, 0, 0)),
            pl.BlockSpec((None, D_MODEL, 2 * MEM_WIDTH), lambda l, i: (l, 0, 0)),
        ],
        out_specs=[
            pl.BlockSpec((None, nb, MEM_WIDTH, N_MEM), lambda l, i: (l, i, 0, 0)),
            pl.BlockSpec((None, nb, MEM_WIDTH, N_MEM), lambda l, i: (l, i, 0, 0)),
        ],
        out_shape=[out, out],
        compiler_params=_params(2),
        name="mem_kv",
    )(mem, g_mem.reshape(depth, 1, D_MODEL), w_mem_kv)


def _softplus(y):
    return jnp.maximum(y, 0.0) + jnp.log1p(jnp.exp(-jnp.abs(y)))


def _gelu_tanh(x):
    c = 0.7978845608028654
    return x * (0.5 * (1.0 + jnp.tanh(c * (x + 0.044715 * (x * x * x)))))


def _mixer_a_kernel(x_ref, perm_ref, perm_t_ref, g_ref, win_ref, cw_ref, cb_ref, wa_ref, wi_ref,
                    ba_ref, bi_ref, lam_ref, conv0_ref, h0_ref,
                    mixq_ref, convs_ref, lrus_ref,
                    xpad_s, h_s, carry_s, *, bb, tt, reset_first):
    t_idx = pl.program_id(1)
    rows = bb * tt
    tsub = MXU_DIM // bb
    hist = (CONV_WIDTH - 1) * bb

    @pl.when(t_idx == 0)
    def _init():
        xpad_s[0:hist, :] = conv0_ref[...].reshape(hist, LRU_WIDTH)
        carry_s[...] = h0_ref[...]

    parts = []
    for s in range(tt // tsub):
        xs = x_ref[:, s * tsub:(s + 1) * tsub, :].reshape(MXU_DIM, D_MODEL)
        hn = _rmsnorm(xs, g_ref[...]).astype(BF16)
        parts.append(_dot(perm_ref[...], hn).astype(BF16))
    hn_tm = jnp.concatenate(parts, axis=0)
    proj = _dot(hn_tm, win_ref[...])
    gate = proj[:, LRU_WIDTH:2 * LRU_WIDTH]

    xpad_s[hist:hist + rows, :] = proj[:, :LRU_WIDTH]
    xc = cb_ref[...] + xpad_s[hist:hist + rows, :] * cw_ref[CONV_WIDTH - 1:CONV_WIDTH, :]
    for k in range(CONV_WIDTH - 1):
        xc = xc + xpad_s[k * bb:k * bb + rows, :] * cw_ref[k:k + 1, :]
    tail = xpad_s[rows:rows + hist, :]
    convs_ref[...] = tail.reshape(CONV_WIDTH - 1, bb, LRU_WIDTH)
    xpad_s[0:hist, :] = tail

    xcb = xc.astype(BF16)
    r_parts, i_parts = [], []
    for gi in range(N_GATE_GROUPS):
        sl = slice(gi * GATE_GROUP, (gi + 1) * GATE_GROUP)
        r_parts.append(_dot(xcb[:, sl], wa_ref[gi]))
        i_parts.append(_dot(xcb[:, sl], wi_ref[gi]))
    r = jax.nn.sigmoid(jnp.concatenate(r_parts, axis=-1) + ba_ref[...])
    ig = jax.nn.sigmoid(jnp.concatenate(i_parts, axis=-1) + bi_ref[...])
    log_a = (-LRU_C * r) * _softplus(-lam_ref[...])
    a = jnp.exp(log_a)
    th = jnp.tanh(log_a)
    mult = jnp.sqrt(-2.0 * th / (1.0 - th))
    if reset_first:
        row = lax.broadcasted_iota(jnp.int32, (rows, 1), 0)
        mult = jnp.where(jnp.logical_and(row < bb, t_idx == 0), 1.0, mult)
    u = mult * (ig * xc)

    h = carry_s[...]
    for t in range(tt):
        h = a[t * bb:(t + 1) * bb, :] * h + u[t * bb:(t + 1) * bb, :]
        h_s[t * bb:(t + 1) * bb, :] = h
    carry_s[...] = h
    lrus_ref[...] = h

    mix = (h_s[...] * _gelu_tanh(gate)).astype(BF16)
    mixq_tm = jnp.concatenate([mix, (proj[:, 2 * LRU_WIDTH:] * Q_SCALE).astype(BF16)], axis=-1)
    for s in range(tt // tsub):
        blk = _dot(perm_t_ref[...], mixq_tm[s * MXU_DIM:(s + 1) * MXU_DIM, :]).astype(BF16)
        mixq_ref[:, s * tsub:(s + 1) * tsub, :] = blk.reshape(bb, tsub, MIX_WIDTH)


def _mixer_a(x, g, w_in, conv_w, conv_b, wa_bd, wi_bd, b_a, b_i, lam, conv0_tm, h0, *, bb, tt,
             reset_first):
    b, t, _ = x.shape
    assert bb == SUBLANES and b % bb == 0 and t % tt == 0 and MXU_DIM % bb == 0
    tsub = MXU_DIM // bb
    assert tt % tsub == 0
    rows = bb * tt
    tm = jnp.arange(MXU_DIM)
    perm = (((tm % bb) * tsub + tm // bb)[:, None] == jnp.arange(MXU_DIM)[None, :]).astype(BF16)
    kern = functools.partial(_mixer_a_kernel, bb=bb, tt=tt, reset_first=reset_first)
    return pl.pallas_call(
        kern,
        grid=(b // bb, t // tt),
        in_specs=[
            pl.BlockSpec((bb, tt, D_MODEL), lambda i, j: (i, j, 0)),
            _const_spec((MXU_DIM, MXU_DIM)),
            _const_spec((MXU_DIM, MXU_DIM)),
            _const_spec((1, D_MODEL)),
            _const_spec((D_MODEL, 2 * LRU_WIDTH + MEM_WIDTH)),
            _const_spec((CONV_WIDTH, LRU_WIDTH)),
            _const_spec((1, LRU_WIDTH)),
            _const_spec((N_GATE_GROUPS, GATE_GROUP, GATE_GROUP)),
            _const_spec((N_GATE_GROUPS, GATE_GROUP, GATE_GROUP)),
            _const_spec((1, LRU_WIDTH)),
            _const_spec((1, LRU_WIDTH)),
            _const_spec((1, LRU_WIDTH)),
            pl.BlockSpec((CONV_WIDTH - 1, bb, LRU_WIDTH), lambda i, j: (0, i, 0)),
            pl.BlockSpec((bb, LRU_WIDTH), lambda i, j: (i, 0)),
        ],
        out_specs=[
            pl.BlockSpec((bb, tt, MIX_WIDTH), lambda i, j: (i, j, 0)),
            pl.BlockSpec((CONV_WIDTH - 1, bb, LRU_WIDTH), lambda i, j: (0, i, 0)),
            pl.BlockSpec((bb, LRU_WIDTH), lambda i, j: (i, 0)),
        ],
        out_shape=[
            jax.ShapeDtypeStruct((b, t, MIX_WIDTH), BF16),
            jax.ShapeDtypeStruct((CONV_WIDTH - 1, b, LRU_WIDTH), F32),
            jax.ShapeDtypeStruct((b, LRU_WIDTH), F32),
        ],
        scratch_shapes=[
            pltpu.VMEM(((CONV_WIDTH - 1) * bb + rows, LRU_WIDTH), F32),
            pltpu.VMEM((rows, LRU_WIDTH), F32),
            pltpu.VMEM((bb, LRU_WIDTH), F32),
        ],
        compiler_params=_params(2),
        name="mixer_a",
    )(x, perm, perm.T, g, w_in, conv_w, conv_b, wa_bd, wi_bd, b_a, b_i, lam, conv0_tm, h0)


def _post_kernel(x_ref, mixq_ref, mkt_ref, mvt_ref, wo_ref, gf_ref, wgu_ref, wd_ref, gt_ref, *rest,
                 bb, tq, tail):
    if tail == 'final':
        y_ref, cat_s = rest
    else:
        wkv_ref, x2_ref, k_out_ref, v_out_ref, k_att_ref, v_att_ref, cat_s = rest
    rows = bb * tq
    mix_w = MIX_WIDTH - MEM_WIDTH

    cat_s[:, :mix_w] = mixq_ref[:, :, :mix_w].reshape(rows, mix_w)
    for b in range(bb):
        q = mixq_ref[b, :, mix_w:].astype(F32)
        s = _group_scores(_stack_heads(q), mkt_ref[b].astype(BF16), k_is_transposed=True)
        e, row_sum = _group_exp(s)
        o = _group_values(e, row_sum, mvt_ref[b].astype(BF16), v_is_transposed=True)
        cat_s[b * tq:(b + 1) * tq, mix_w:] = o.astype(BF16)

    x = x_ref[...].reshape(rows, D_MODEL)
    x1 = x + _dot(cat_s[...], wo_ref[...])
    hf = _rmsnorm(x1, gf_ref[...]).astype(BF16)
    acts = []
    for c0, c1 in FF_CHUNKS:
        gg = _dot(hf, wgu_ref[:, c0:c1])
        uu = _dot(hf, wgu_ref[:, D_FF + c0:D_FF + c1])
        acts.append(((gg * jax.nn.sigmoid(gg)) * uu).astype(BF16))
    x2 = x1 + _dot(jnp.concatenate(acts, axis=-1), wd_ref[...])

    if tail == 'final':
        y_ref[...] = _rmsnorm(x2, gt_ref[...]).reshape(bb, tq, D_MODEL)
        return
    x2_ref[...] = x2.reshape(bb, tq, D_MODEL)
    hk = _rmsnorm(x2, gt_ref[...]).astype(BF16)
    kv = _dot(hk, wkv_ref[...])
    k = kv[:, :ATT_WIDTH]
    v = kv[:, ATT_WIDTH:]
    if tail == 'kv_prompt':
        k_att_ref[...] = k.reshape(bb, tq, ATT_WIDTH).astype(BF16)
        v_att_ref[...] = v.reshape(bb, tq, ATT_WIDTH).astype(BF16)

        @pl.when(pl.program_id(1) == pl.num_programs(1) - 1)
        def _write_cache():
            k_out_ref[0] = k.T
            v_out_ref[0] = v.T
    else:
        for b in range(bb):
            for h in range(N_ATT_HEADS):
                hs = slice(h * HEAD_DIM, (h + 1) * HEAD_DIM)
                k_out_ref[b, h] = k[b * tq:(b + 1) * tq, hs]
                v_out_ref[b, h] = v[b * tq:(b + 1) * tq, hs]
        k_att_ref[...] = k.T.astype(BF16)
        v_att_ref[...] = v.T.astype(BF16)


def _post(x, mixq, mem_kt, mem_vt, layer, w_out, g_ffn, w_gu, w_d, g_tail, w_kv=None, *, bb, tq, tail):
    b, t, _ = x.shape
    assert b % bb == 0 and t % tq == 0
    tile = lambda w: pl.BlockSpec((bb, tq, w), lambda i, j: (i, j, 0))
    mem_spec = pl.BlockSpec((None, bb, MEM_WIDTH, N_MEM), lambda i, j: (layer, i, 0, 0))
    in_specs = [
        tile(D_MODEL), tile(MIX_WIDTH), mem_spec, mem_spec,
        _layer_spec((MIX_WIDTH, D_MODEL), layer),
        _layer_spec((1, D_MODEL), layer),
        _layer_spec((D_MODEL, 2 * D_FF), layer),
        _layer_spec((D_FF, D_MODEL), layer),
        _const_spec((1, D_MODEL)),
    ]
    args = [x, mixq, mem_kt, mem_vt, w_out, g_ffn, w_gu, w_d, g_tail]
    if tail == 'final':
        out_specs = [tile(D_MODEL)]
        out_shape = [jax.ShapeDtypeStruct((b, t, D_MODEL), F32)]
    else:
        in_specs.append(_const_spec((D_MODEL, 2 * ATT_WIDTH)))
        args.append(w_kv)
        if tail == 'kv_prompt':
            assert bb == 1 and tq == BAND_ROWS
            cache_spec = pl.BlockSpec((1, ATT_WIDTH, BAND_ROWS), lambda i, j: (i, 0, 0))
            out_specs = [tile(D_MODEL), cache_spec, cache_spec, tile(ATT_WIDTH), tile(ATT_WIDTH)]
            out_shape = [jax.ShapeDtypeStruct((b, t, D_MODEL), F32),
                         jax.ShapeDtypeStruct((b, ATT_WIDTH, BAND_ROWS), F32),
                         jax.ShapeDtypeStruct((b, ATT_WIDTH, BAND_ROWS), F32),
                         jax.ShapeDtypeStruct((b, t, ATT_WIDTH), BF16),
                         jax.ShapeDtypeStruct((b, t, ATT_WIDTH), BF16)]
        else:
            assert tq == t
            head_spec = pl.BlockSpec((bb, N_ATT_HEADS, t, HEAD_DIM), lambda i, j: (i, 0, 0, 0),
                                     pipeline_mode=pl.Buffered(1))
            kt_spec = pl.BlockSpec((None, ATT_WIDTH, bb * t), lambda i, j: (i, 0, 0))
            out_specs = [tile(D_MODEL), head_spec, head_spec, kt_spec, kt_spec]
            out_shape = [jax.ShapeDtypeStruct((b, t, D_MODEL), F32),
                         jax.ShapeDtypeStruct((b, N_ATT_HEADS, t, HEAD_DIM), F32),
                         jax.ShapeDtypeStruct((b, N_ATT_HEADS, t, HEAD_DIM), F32),
                         jax.ShapeDtypeStruct((b // bb, ATT_WIDTH, bb * t), BF16),
                         jax.ShapeDtypeStruct((b // bb, ATT_WIDTH, bb * t), BF16)]
    kern = functools.partial(_post_kernel, bb=bb, tq=tq, tail=tail)
    return pl.pallas_call(
        kern,
        grid=(b // bb, t // tq),
        in_specs=in_specs,
        out_specs=out_specs,
        out_shape=out_shape,
        scratch_shapes=[pltpu.VMEM((bb * tq, MIX_WIDTH), BF16)],
        compiler_params=_params(2),
        name="post_" + tail,
    )(*args)


def _build_band_bias(tab_ref, bias_s, tq):
    kw = BAND_ROWS + tq
    r_io = lax.broadcasted_iota(jnp.int32, (tq, kw), 0)
    i_io = lax.broadcasted_iota(jnp.int32, (tq, kw), 1)
    off = i_io - jnp.bitwise_and(r_io, -CHUNK)
    in_band = jnp.logical_and(off >= 0, off < BAND_KEYS)
    for h in range(N_ATT_HEADS):
        row = jnp.broadcast_to(tab_ref[h:h + 1, :], (tq, BIAS_TABLE_LEN))
        rolled = pltpu.roll(row, 0, 1, stride=1, stride_axis=0)[:, :kw]
        bias_s[h * tq:(h + 1) * tq, :] = jnp.where(in_band, rolled * LOG2_E, NEG_INF)


def _project_queries(x_ref, g_ref, win_ref, mixq_ref, qst_s, bb, tq):
    rows = bb * tq
    x = x_ref[...].reshape(rows, D_MODEL)
    hn = _rmsnorm(x, g_ref[...]).astype(BF16)
    proj = _dot(hn, win_ref[...])
    mixq_ref[:, :, ATT_WIDTH:] = (proj[:, ATT_WIDTH:] * Q_SCALE).reshape(bb, tq, MEM_WIDTH).astype(BF16)
    lane_head = _lane_head(rows)
    for g in range(N_ATT_GROUPS):
        qg = proj[:, g * MXU_DIM:(g + 1) * MXU_DIM] * Q_SCALE
        for h in range(HEADS_PER_GROUP):
            blk = jnp.where(lane_head == h, qg, 0.0).astype(BF16)
            qst_s[:, g, h * tq:(h + 1) * tq, :] = blk.reshape(bb, tq, MXU_DIM)


def _mixer_b_sample_kernel(x_ref, g_ref, win_ref, ktn_ref, vtn_ref, pkt_ref, pvt_ref, tab_ref,
                           mixq_ref, kt_s, vt_s, bias_s, qst_s, *, bb, tq):
    group_rows = HEADS_PER_GROUP * tq

    @pl.when(pl.program_id(0) == 0)
    def _build_bias():
        _build_band_bias(tab_ref, bias_s, tq)

    kt_s[:, :, :BAND_ROWS] = pkt_ref[...].astype(BF16)
    vt_s[:, :, :BAND_ROWS] = pvt_ref[...].astype(BF16)
    for b in range(bb):
        kt_s[b, :, BAND_ROWS:] = ktn_ref[:, b * tq:(b + 1) * tq]
        vt_s[b, :, BAND_ROWS:] = vtn_ref[:, b * tq:(b + 1) * tq]
    _project_queries(x_ref, g_ref, win_ref, mixq_ref, qst_s, bb, tq)

    def one_batch(b, carry):
        for g in range(N_ATT_GROUPS):
            ls = slice(g * MXU_DIM, (g + 1) * MXU_DIM)
            s = _group_scores(qst_s[b, g], kt_s[b, ls, :], bias_s[g * group_rows:(g + 1) * group_rows, :],
                              k_is_transposed=True)
            e, row_sum = _group_exp(s)
            out = _group_values(e, row_sum, vt_s[b, ls, :], v_is_transposed=True)
            mixq_ref[b, :, ls] = out.astype(BF16)
        return carry

    lax.fori_loop(0, bb, one_batch, 0)


def _mixer_b_sample(x, g, w_in, kt_new, vt_new, past_kt, past_vt, table, *, bb):
    b, t, _ = x.shape
    tq = t
    nb = kt_new.shape[2] // t
    assert t == CHUNK and b % bb == 0 and nb % bb == 0 and (bb * t) % 128 == 0
    per = nb // bb
    kw = BAND_ROWS + tq
    new_spec = pl.BlockSpec((None, ATT_WIDTH, bb * t), lambda i: (i // per, 0, i % per))
    past_spec = pl.BlockSpec((bb, ATT_WIDTH, BAND_ROWS), lambda i: (i, 0, 0))
    return pl.pallas_call(
        functools.partial(_mixer_b_sample_kernel, bb=bb, tq=tq),
        grid=(b // bb,),
        in_specs=[
            pl.BlockSpec((bb, tq, D_MODEL), lambda i: (i, 0, 0)),
            _const_spec((1, D_MODEL)),
            _const_spec((D_MODEL, ATT_WIDTH + MEM_WIDTH)),
            new_spec, new_spec, past_spec, past_spec,
            _const_spec((N_ATT_HEADS, BIAS_TABLE_LEN)),
        ],
        out_specs=pl.BlockSpec((bb, tq, MIX_WIDTH), lambda i: (i, 0, 0)),
        out_shape=jax.ShapeDtypeStruct((b, t, MIX_WIDTH), BF16),
        scratch_shapes=[
            pltpu.VMEM((bb, ATT_WIDTH, kw), BF16),
            pltpu.VMEM((bb, ATT_WIDTH, kw), BF16),
            pltpu.VMEM((N_ATT_HEADS * tq, kw), F32),
            pltpu.VMEM((bb, N_ATT_GROUPS, HEADS_PER_GROUP * tq, MXU_DIM), BF16),
        ],
        compiler_params=_params(1),
        name="mixer_b_sample",
    )(x, g, w_in, kt_new, vt_new, past_kt, past_vt, table)


def _mixer_b_kernel(x_ref, g_ref, win_ref, kb_ref, vb_ref, tab_ref, mixq_ref, kpad_s, vpad_s, bias_s,
                    qst_s, *, tq):
    bb = 1
    bi = pl.program_id(0)
    j = pl.program_id(1)
    kw = BAND_ROWS + tq
    group_rows = HEADS_PER_GROUP * tq

    @pl.when(jnp.logical_and(bi == 0, j == 0))
    def _build_bias():
        _build_band_bias(tab_ref, bias_s, tq)

    @pl.when(j == 0)
    def _fill_kv():
        kpad_s[:, :BAND_ROWS, :] = jnp.zeros((bb, BAND_ROWS, ATT_WIDTH), BF16)
        vpad_s[:, :BAND_ROWS, :] = jnp.zeros((bb, BAND_ROWS, ATT_WIDTH), BF16)
        kpad_s[:, BAND_ROWS:, :] = kb_ref[...]
        vpad_s[:, BAND_ROWS:, :] = vb_ref[...]

    _project_queries(x_ref, g_ref, win_ref, mixq_ref, qst_s, bb, tq)
    start = pl.multiple_of(j * tq, tq)

    def attend(mask_keys):
        key_ok = None
        if mask_keys:
            key_ok = lax.broadcasted_iota(jnp.int32, (1, kw), 1) >= BAND_ROWS - j * tq

        def one_batch(b, carry):
            lanes = [slice(g * MXU_DIM, (g + 1) * MXU_DIM) for g in range(N_ATT_GROUPS)]

            def scores(g):
                return _group_scores(qst_s[b, g], kpad_s[b, pl.ds(start, kw), lanes[g]],
                                     bias_s[g * group_rows:(g + 1) * group_rows, :], key_ok)

            s_next = scores(0)
            for g in range(N_ATT_GROUPS):
                s_cur = s_next
                if g + 1 < N_ATT_GROUPS:
                    s_next = scores(g + 1)
                e, row_sum = _group_exp(s_cur)
                out = _group_values(e, row_sum, vpad_s[b, pl.ds(start, kw), lanes[g]])
                mixq_ref[b, :, lanes[g]] = out.astype(BF16)
            return carry

        one_batch(0, 0)

    n_masked_tiles = BAND_ROWS // tq
    pl.when(j < n_masked_tiles)(lambda: attend(True))
    pl.when(j >= n_masked_tiles)(lambda: attend(False))


def _mixer_b(x, g, w_in, kb, vb, table, *, tq):
    b, t, _ = x.shape
    assert t % tq == 0 and tq % CHUNK == 0 and BAND_ROWS % tq == 0
    assert BAND_ROWS + 2 * tq - 1 <= BIAS_TABLE_LEN
    kw = BAND_ROWS + tq
    return pl.pallas_call(
        functools.partial(_mixer_b_kernel, tq=tq),
        grid=(b, t // tq),
        in_specs=[
            pl.BlockSpec((1, tq, D_MODEL), lambda i, j: (i, j, 0)),
            _const_spec((1, D_MODEL)),
            _const_spec((D_MODEL, ATT_WIDTH + MEM_WIDTH)),
            pl.BlockSpec((1, t, ATT_WIDTH), lambda i, j: (i, 0, 0)),
            pl.BlockSpec((1, t, ATT_WIDTH), lambda i, j: (i, 0, 0)),
            _const_spec((N_ATT_HEADS, BIAS_TABLE_LEN)),
        ],
        out_specs=pl.BlockSpec((1, tq, MIX_WIDTH), lambda i, j: (i, j, 0)),
        out_shape=jax.ShapeDtypeStruct((b, t, MIX_WIDTH), BF16),
        scratch_shapes=[
            pltpu.VMEM((1, BAND_ROWS + t, ATT_WIDTH), BF16),
            pltpu.VMEM((1, BAND_ROWS + t, ATT_WIDTH), BF16),
            pltpu.VMEM((N_ATT_HEADS * tq, kw), F32),
            pltpu.VMEM((1, N_ATT_GROUPS, HEADS_PER_GROUP * tq, MXU_DIM), BF16),
        ],
        compiler_params=_params(2),
        name="mixer_b",
    )(x, g, w_in, kb, vb, table)


def _block_diag_groups(w):
    per = GATE_GROUP // HEAD_DIM
    w4 = w.reshape(N_GATE_GROUPS, per, HEAD_DIM, HEAD_DIM)
    eye = jnp.eye(per, dtype=w.dtype)
    bd = w4[:, :, :, None, :] * eye[None, :, None, :, None]
    return bd.reshape(N_GATE_GROUPS, GATE_GROUP, GATE_GROUP)


def _bias_table(rel):
    top = rel[-1:]
    head = jnp.broadcast_to(top, (BAND_ROWS - REL_CLIP, rel.shape[1]))
    tail = jnp.broadcast_to(top, (BIAS_TABLE_LEN - (BAND_ROWS - REL_CLIP) - rel.shape[0], rel.shape[1]))
    return jnp.concatenate([head, rel[::-1], tail], axis=0).T


def _trunk(x, is_prompt, conv0, h0, past_kt, past_vt, mem_kt, mem_vt, p, *, a_tile, post_tile, b_tile):
    b, t, _ = x.shape
    mixq, conv_s, lru_s = _mixer_a(
        x, p['g_mix'][0], p['w_in_a'], p['conv_w'], p['conv_b'], p['wa_bd'], p['wi_bd'], p['b_rg_a'],
        p['b_rg_i'], p['lam'], jnp.swapaxes(conv0, 0, 1), h0, bb=a_tile[0], tt=a_tile[1],
        reset_first=is_prompt)
    x, k_out, v_out, k_att, v_att = _post(
        x, mixq, mem_kt, mem_vt, 0, p['w_out'], p['g_ffn'], p['w_gu'], p['w_d'], p['g_kv'], p['w_kv'],
        bb=post_tile[0], tq=post_tile[1], tail='kv_prompt' if is_prompt else 'kv_sample')
    if is_prompt:
        mixq = _mixer_b(x, p['g_mix'][1], p['w_in_b'], k_att, v_att, p['bias_table'], tq=b_tile[1])
        k_new = jnp.transpose(k_out.reshape(b, N_ATT_HEADS, HEAD_DIM, -1), (0, 3, 1, 2))
        v_new = jnp.transpose(v_out.reshape(b, N_ATT_HEADS, HEAD_DIM, -1), (0, 3, 1, 2))
    else:
        mixq = _mixer_b_sample(x, p['g_mix'][1], p['w_in_b'], k_att, v_att, past_kt, past_vt,
                               p['bias_table'], bb=b_tile[0])
        k_new = jnp.transpose(k_out, (0, 2, 1, 3))
        v_new = jnp.transpose(v_out, (0, 2, 1, 3))
    (y,) = _post(
        x, mixq, mem_kt, mem_vt, 1, p['w_out'], p['g_ffn'], p['w_gu'], p['w_d'], p['g_final'],
        bb=post_tile[0], tq=post_tile[1], tail='final')
    return y, jnp.swapaxes(conv_s, 0, 1)[None], lru_s[None], k_new, v_new


def kernel(x_prompt, x_sample, state_conv, state_lru, cache_k, cache_v, cache_mem_k, cache_mem_v, mem_prompt, g_mix, g_ffn, g_final, w_in_a, conv_w, conv_b, w_rg_a, b_rg_a, w_rg_i, b_rg_i, lru_lambda, g_kv, w_kv, w_in_b, rel_bias, g_mem, w_mem_kv, w_out, w_ffn_gu, w_ffn_down):
    depth = g_mix.shape[0]
    row = lambda v: v.reshape(1, -1)
    p = {
        'g_mix': [row(g_mix[l]) for l in range(depth)],
        'g_ffn': g_ffn.reshape(depth, 1, D_MODEL),
        'g_final': row(g_final), 'g_kv': row(g_kv),
        'w_in_a': w_in_a[0].astype(BF16), 'conv_w': conv_w[0], 'conv_b': row(conv_b[0]),
        'wa_bd': _block_diag_groups(w_rg_a[0]).astype(BF16),
        'wi_bd': _block_diag_groups(w_rg_i[0]).astype(BF16),
        'b_rg_a': row(b_rg_a[0]), 'b_rg_i': row(b_rg_i[0]), 'lam': row(lru_lambda[0]),
        'w_kv': w_kv.astype(BF16),
        'w_in_b': w_in_b[0].astype(BF16), 'bias_table': _bias_table(rel_bias[0]),
        'w_out': w_out.astype(BF16), 'w_gu': w_ffn_gu.astype(BF16), 'w_d': w_ffn_down.astype(BF16),
    }
    bp = x_prompt.shape[0]
    bs = x_sample.shape[0]

    def keys_minor(a):
        nd = a.ndim
        a = jnp.transpose(a, tuple(range(nd - 3)) + (nd - 2, nd - 1, nd - 3))
        return a.reshape(a.shape[:nd - 3] + (a.shape[nd - 3] * a.shape[nd - 2], a.shape[nd - 1]))

    def keys_major(a, heads):
        nd = a.ndim
        a = a.reshape(a.shape[:nd - 2] + (heads, HEAD_DIM, a.shape[nd - 1]))
        return jnp.transpose(a, tuple(range(nd - 2)) + (nd, nd - 2, nd - 1))

    mem_kt_p, mem_vt_p = _mem_kv(mem_prompt, g_mem, w_mem_kv.astype(BF16))
    conv0 = jnp.zeros((bp, CONV_WIDTH - 1, LRU_WIDTH), F32)
    lru0 = jnp.zeros((bp, LRU_WIDTH), F32)
    y_p, conv_p, lru_p, k_p, v_p = _trunk(
        x_prompt, True, conv0, lru0, None, None, mem_kt_p, mem_vt_p, p,
        a_tile=(8, 64), post_tile=(1, 512), b_tile=(1, 256))

    y_s, conv_s, lru_s, k_s, v_s = _trunk(
        x_sample, False, state_conv[0], state_lru[0], keys_minor(cache_k), keys_minor(cache_v),
        keys_minor(cache_mem_k), keys_minor(cache_mem_v), p,
        a_tile=(8, 64), post_tile=(8, 64), b_tile=(4, 64))

    return (y_p, y_s, conv_p, lru_p, k_p, v_p, keys_major(mem_kt_p, N_MEM_HEADS),
            keys_major(mem_vt_p, N_MEM_HEADS), conv_s, lru_s, k_s, v_s)
```

```python
import functools

import jax
import jax.numpy as jnp
from jax import lax
from jax.experimental import pallas as pl
from jax.experimental.pallas import tpu as pltpu

F32 = jnp.float32
BF16 = jnp.bfloat16

D_MODEL = 1024
HEAD_DIM = 64
LRU_WIDTH = 768
ATT_WIDTH = 768
N_ATT_HEADS = ATT_WIDTH // HEAD_DIM
MEM_WIDTH = 256
N_MEM_HEADS = MEM_WIDTH // HEAD_DIM
MIX_WIDTH = LRU_WIDTH + MEM_WIDTH
N_MEM = 256
D_FF = 2816
CONV_WIDTH = 4
CHUNK = 64
BAND_ROWS = 512
BAND_KEYS = BAND_ROWS + CHUNK
REL_CLIP = 256
LRU_C = 8.0
RMS_EPS = 1e-6
NEG_INF = -1e30
ATT_SCALE = HEAD_DIM ** -0.5
LOG2_E = 1.4426950408889634
Q_SCALE = ATT_SCALE * LOG2_E

SUBLANES = 8
MXU_DIM = 256
HEADS_PER_GROUP = MXU_DIM // HEAD_DIM
N_ATT_GROUPS = ATT_WIDTH // MXU_DIM
GATE_GROUP = MXU_DIM
N_GATE_GROUPS = LRU_WIDTH // GATE_GROUP
BIAS_TABLE_LEN = 1024
FF_CHUNKS = ((0, 1024), (1024, 2048), (2048, D_FF))
VMEM_LIMIT_BYTES = 58 * 1024 * 1024


def _rmsnorm(x, g):
    return x * lax.rsqrt(jnp.mean(x * x, axis=-1, keepdims=True) + RMS_EPS) * g


def _dot(a, b):
    return jnp.dot(a, b, preferred_element_type=F32)


def _dot_nt(a, b):
    return lax.dot_general(a, b, (((1,), (1,)), ((), ())), preferred_element_type=F32)


def _const_spec(shape):
    nd = len(shape)
    return pl.BlockSpec(shape, lambda *_: (0,) * nd, pipeline_mode=pl.Buffered(1))


def _layer_spec(shape, layer):
    nd = len(shape)
    return pl.BlockSpec((None,) + tuple(shape), lambda *_: (layer,) + (0,) * nd,
                        pipeline_mode=pl.Buffered(1))


def _lane_head(tq):
    return lax.broadcasted_iota(jnp.int32, (tq, MXU_DIM), 1) // HEAD_DIM


def _stack_heads(q):
    lane_head = _lane_head(q.shape[0])
    return jnp.concatenate(
        [jnp.where(lane_head == h, q, 0.0) for h in range(HEADS_PER_GROUP)], axis=0).astype(BF16)


def _group_scores(q_stack, k, bias=None, key_ok=None, *, k_is_transposed=False):
    s = _dot(q_stack, k) if k_is_transposed else _dot_nt(q_stack, k)
    if bias is not None:
        s = s + bias
    if key_ok is not None:
        s = jnp.where(key_ok, s, NEG_INF)
    return s


def _group_exp(s):
    e = jnp.exp2(s - jnp.max(s, axis=-1, keepdims=True))
    return e.astype(BF16), jnp.sum(e, axis=-1, keepdims=True)


def _group_values(e, row_sum, v, *, v_is_transposed=False):
    tq = e.shape[0] // HEADS_PER_GROUP
    o = (_dot_nt(e, v) if v_is_transposed else _dot(e, v)) / row_sum
    lane_head = _lane_head(tq)
    out = jnp.where(lane_head == 0, o[:tq], 0.0)
    for h in range(1, HEADS_PER_GROUP):
        out = out + jnp.where(lane_head == h, o[h * tq:(h + 1) * tq], 0.0)
    return out


def _params(n_axes):
    return pltpu.CompilerParams(dimension_semantics=("arbitrary",) * n_axes,
                                vmem_limit_bytes=VMEM_LIMIT_BYTES)


def _mem_kv_kernel(x_ref, g_ref, w_ref, kt_ref, vt_ref, *, nb):
    h = _rmsnorm(x_ref[...].reshape(nb * N_MEM, D_MODEL), g_ref[...]).astype(BF16)
    kv = _dot(h, w_ref[...])
    for b in range(nb):
        blk = kv[b * N_MEM:(b + 1) * N_MEM, :]
        kt_ref[b] = blk[:, :MEM_WIDTH].T
        vt_ref[b] = blk[:, MEM_WIDTH:].T


def _mem_kv(mem, g_mem, w_mem_kv):
    depth = g_mem.shape[0]
    b = mem.shape[0]
    nb = 2
    assert b % nb == 0
    out = jax.ShapeDtypeStruct((depth, b, MEM_WIDTH, N_MEM), F32)
    return pl.pallas_call(
        functools.partial(_mem_kv_kernel, nb=nb),
        grid=(depth, b // nb),
        in_specs=[
            pl.BlockSpec((nb, N_MEM, D_MODEL), lambda l, i: (i, 0, 0)),
            pl.BlockSpec((None, 1, D_MODEL), lambda l, i: (l, 0, 0)),
            pl.BlockSpec((None, D_MODEL, 2 * MEM_WIDTH), lambda l, i: (l, 0, 0)),
        ],
        out_specs=[
            pl.BlockSpec((None, nb, MEM_WIDTH, N_MEM), lambda l, i: (l, i, 0, 0)),
            pl.BlockSpec((None, nb, MEM_WIDTH, N_MEM), lambda l, i: (l, i, 0, 0)),
        ],
        out_shape=[out, out],
        compiler_params=_params(2),
        name="mem_kv",
    )(mem, g_mem.reshape(depth, 1, D_MODEL), w_mem_kv)


def _softplus(y):
    return jnp.maximum(y, 0.0) + jnp.log1p(jnp.exp(-jnp.abs(y)))


def _gelu_tanh(x):
    c = 0.7978845608028654
    return x * (0.5 * (1.0 + jnp.tanh(c * (x + 0.044715 * (x * x * x)))))


def _mixer_a_kernel(x_ref, perm_ref, perm_t_ref, g_ref, win_ref, cw_ref, cb_ref, wa_ref, wi_ref,
                    ba_ref, bi_ref, lam_ref, conv0_ref, h0_ref,
                    mixq_ref, convs_ref, lrus_ref,
                    xpad_s, h_s, carry_s, *, bb, tt, reset_first):
    t_idx = pl.program_id(1)
    rows = bb * tt
    tsub = MXU_DIM // bb
    hist = (CONV_WIDTH - 1) * bb

    @pl.when(t_idx == 0)
    def _init():
        xpad_s[0:hist, :] = conv0_ref[...].reshape(hist, LRU_WIDTH)
        carry_s[...] = h0_ref[...]

    parts = []
    for s in range(tt // tsub):
        xs = x_ref[:, s * tsub:(s + 1) * tsub, :].reshape(MXU_DIM, D_MODEL)
        hn = _rmsnorm(xs, g_ref[...]).astype(BF16)
        parts.append(_dot(perm_ref[...], hn).astype(BF16))
    hn_tm = jnp.concatenate(parts, axis=0)
    proj = _dot(hn_tm, win_ref[...])
    gate = proj[:, LRU_WIDTH:2 * LRU_WIDTH]

    xpad_s[hist:hist + rows, :] = proj[:, :LRU_WIDTH]
    xc = cb_ref[...] + xpad_s[hist:hist + rows, :] * cw_ref[CONV_WIDTH - 1:CONV_WIDTH, :]
    for k in range(CONV_WIDTH - 1):
        xc = xc + xpad_s[k * bb:k * bb + rows, :] * cw_ref[k:k + 1, :]
    tail = xpad_s[rows:rows + hist, :]
    convs_ref[...] = tail.reshape(CONV_WIDTH - 1, bb, LRU_WIDTH)
    xpad_s[0:hist, :] = tail

    xcb = xc.astype(BF16)
    r_parts, i_parts = [], []
    for gi in range(N_GATE_GROUPS):
        sl = slice(gi * GATE_GROUP, (gi + 1) * GATE_GROUP)
        r_parts.append(_dot(xcb[:, sl], wa_ref[gi]))
        i_parts.append(_dot(xcb[:, sl], wi_ref[gi]))
    r = jax.nn.sigmoid(jnp.concatenate(r_parts, axis=-1) + ba_ref[...])
    ig = jax.nn.sigmoid(jnp.concatenate(i_parts, axis=-1) + bi_ref[...])
    log_a = (-LRU_C * r) * _softplus(-lam_ref[...])
    a = jnp.exp(log_a)
    th = jnp.tanh(log_a)
    mult = jnp.sqrt(-2.0 * th / (1.0 - th))
    if reset_first:
        row = lax.broadcasted_iota(jnp.int32, (rows, 1), 0)
        mult = jnp.where(jnp.logical_and(row < bb, t_idx == 0), 1.0, mult)
    u = mult * (ig * xc)

    h = carry_s[...]
    for t in range(tt):
        h = a[t * bb:(t + 1) * bb, :] * h + u[t * bb:(t + 1) * bb, :]
        h_s[t * bb:(t + 1) * bb, :] = h
    carry_s[...] = h
    lrus_ref[...] = h

    mix = (h_s[...] * _gelu_tanh(gate)).astype(BF16)
    mixq_tm = jnp.concatenate([mix, (proj[:, 2 * LRU_WIDTH:] * Q_SCALE).astype(BF16)], axis=-1)
    for s in range(tt // tsub):
        blk = _dot(perm_t_ref[...], mixq_tm[s * MXU_DIM:(s + 1) * MXU_DIM, :]).astype(BF16)
        mixq_ref[:, s * tsub:(s + 1) * tsub, :] = blk.reshape(bb, tsub, MIX_WIDTH)


def _mixer_a(x, g, w_in, conv_w, conv_b, wa_bd, wi_bd, b_a, b_i, lam, conv0_tm, h0, *, bb, tt,
             reset_first):
    b, t, _ = x.shape
    assert bb == SUBLANES and b % bb == 0 and t % tt == 0 and MXU_DIM % bb == 0
    tsub = MXU_DIM // bb
    assert tt % tsub == 0
    rows = bb * tt
    tm = jnp.arange(MXU_DIM)
    perm = (((tm % bb) * tsub + tm // bb)[:, None] == jnp.arange(MXU_DIM)[None, :]).astype(BF16)
    kern = functools.partial(_mixer_a_kernel, bb=bb, tt=tt, reset_first=reset_first)
    return pl.pallas_call(
        kern,
        grid=(b // bb, t // tt),
        in_specs=[
            pl.BlockSpec((bb, tt, D_MODEL), lambda i, j: (i, j, 0)),
            _const_spec((MXU_DIM, MXU_DIM)),
            _const_spec((MXU_DIM, MXU_DIM)),
            _const_spec((1, D_MODEL)),
            _const_spec((D_MODEL, 2 * LRU_WIDTH + MEM_WIDTH)),
            _const_spec((CONV_WIDTH, LRU_WIDTH)),
            _const_spec((1, LRU_WIDTH)),
            _const_spec((N_GATE_GROUPS, GATE_GROUP, GATE_GROUP)),
            _const_spec((N_GATE_GROUPS, GATE_GROUP, GATE_GROUP)),
            _const_spec((1, LRU_WIDTH)),
            _const_spec((1, LRU_WIDTH)),
            _const_spec((1, LRU_WIDTH)),
            pl.BlockSpec((CONV_WIDTH - 1, bb, LRU_WIDTH), lambda i, j: (0, i, 0)),
            pl.BlockSpec((bb, LRU_WIDTH), lambda i, j: (i, 0)),
        ],
        out_specs=[
            pl.BlockSpec((bb, tt, MIX_WIDTH), lambda i, j: (i, j, 0)),
            pl.BlockSpec((CONV_WIDTH - 1, bb, LRU_WIDTH), lambda i, j: (0, i, 0)),
            pl.BlockSpec((bb, LRU_WIDTH), lambda i, j: (i, 0)),
        ],
        out_shape=[
            jax.ShapeDtypeStruct((b, t, MIX_WIDTH), BF16),
            jax.ShapeDtypeStruct((CONV_WIDTH - 1, b, LRU_WIDTH), F32),
            jax.ShapeDtypeStruct((b, LRU_WIDTH), F32),
        ],
        scratch_shapes=[
            pltpu.VMEM(((CONV_WIDTH - 1) * bb + rows, LRU_WIDTH), F32),
            pltpu.VMEM((rows, LRU_WIDTH), F32),
            pltpu.VMEM((bb, LRU_WIDTH), F32),
        ],
        compiler_params=_params(2),
        name="mixer_a",
    )(x, perm, perm.T, g, w_in, conv_w, conv_b, wa_bd, wi_bd, b_a, b_i, lam, conv0_tm, h0)


def _post_kernel(x_ref, mixq_ref, mkt_ref, mvt_ref, wo_ref, gf_ref, wgu_ref, wd_ref, gt_ref, *rest,
                 bb, tq, tail):
    if tail == 'final':
        y_ref, cat_s = rest
    else:
        wkv_ref, x2_ref, k_out_ref, v_out_ref, k_att_ref, v_att_ref, cat_s = rest
    rows = bb * tq
    mix_w = MIX_WIDTH - MEM_WIDTH

    cat_s[:, :mix_w] = mixq_ref[:, :, :mix_w].reshape(rows, mix_w)
    for b in range(bb):
        q = mixq_ref[b, :, mix_w:].astype(F32)
        s = _group_scores(_stack_heads(q), mkt_ref[b].astype(BF16), k_is_transposed=True)
        e, row_sum = _group_exp(s)
        o = _group_values(e, row_sum, mvt_ref[b].astype(BF16), v_is_transposed=True)
        cat_s[b * tq:(b + 1) * tq, mix_w:] = o.astype(BF16)

    x = x_ref[...].reshape(rows, D_MODEL)
    x1 = x + _dot(cat_s[...], wo_ref[...])
    hf = _rmsnorm(x1, gf_ref[...]).astype(BF16)
    acts = []
    for c0, c1 in FF_CHUNKS:
        gg = _dot(hf, wgu_ref[:, c0:c1])
        uu = _dot(hf, wgu_ref[:, D_FF + c0:D_FF + c1])
        acts.append(((gg * jax.nn.sigmoid(gg)) * uu).astype(BF16))
    x2 = x1 + _dot(jnp.concatenate(acts, axis=-1), wd_ref[...])

    if tail == 'final':
        y_ref[...] = _rmsnorm(x2, gt_ref[...]).reshape(bb, tq, D_MODEL)
        return
    x2_ref[...] = x2.reshape(bb, tq, D_MODEL)
    hk = _rmsnorm(x2, gt_ref[...]).astype(BF16)
    kv = _dot(hk, wkv_ref[...])
    k = kv[:, :ATT_WIDTH]
    v = kv[:, ATT_WIDTH:]
    if tail == 'kv_prompt':
        k_att_ref[...] = k.reshape(bb, tq, ATT_WIDTH).astype(BF16)
        v_att_ref[...] = v.reshape(bb, tq, ATT_WIDTH).astype(BF16)

        @pl.when(pl.program_id(1) == pl.num_programs(1) - 1)
        def _write_cache():
            k_out_ref[0] = k.T
            v_out_ref[0] = v.T
    else:
        for b in range(bb):
            for h in range(N_ATT_HEADS):
                hs = slice(h * HEAD_DIM, (h + 1) * HEAD_DIM)
                k_out_ref[b, h] = k[b * tq:(b + 1) * tq, hs]
                v_out_ref[b, h] = v[b * tq:(b + 1) * tq, hs]
        k_att_ref[...] = k.T.astype(BF16)
        v_att_ref[...] = v.T.astype(BF16)


def _post(x, mixq, mem_kt, mem_vt, layer, w_out, g_ffn, w_gu, w_d, g_tail, w_kv=None, *, bb, tq, tail):
    b, t, _ = x.shape
    assert b % bb == 0 and t % tq == 0
    tile = lambda w: pl.BlockSpec((bb, tq, w), lambda i, j: (i, j, 0))
    mem_spec = pl.BlockSpec((None, bb, MEM_WIDTH, N_MEM), lambda i, j: (layer, i, 0, 0))
    in_specs = [
        tile(D_MODEL), tile(MIX_WIDTH), mem_spec, mem_spec,
        _layer_spec((MIX_WIDTH, D_MODEL), layer),
        _layer_spec((1, D_MODEL), layer),
        _layer_spec((D_MODEL, 2 * D_FF), layer),
        _layer_spec((D_FF, D_MODEL), layer),
        _const_spec((1, D_MODEL)),
    ]
    args = [x, mixq, mem_kt, mem_vt, w_out, g_ffn, w_gu, w_d, g_tail]
    if tail == 'final':
        out_specs = [tile(D_MODEL)]
        out_shape = [jax.ShapeDtypeStruct((b, t, D_MODEL), F32)]
    else:
        in_specs.append(_const_spec((D_MODEL, 2 * ATT_WIDTH)))
        args.append(w_kv)
        if tail == 'kv_prompt':
            assert bb == 1 and tq == BAND_ROWS
            cache_spec = pl.BlockSpec((1, ATT_WIDTH, BAND_ROWS), lambda i, j: (i, 0, 0))
            out_specs = [tile(D_MODEL), cache_spec, cache_spec, tile(ATT_WIDTH), tile(ATT_WIDTH)]
            out_shape = [jax.ShapeDtypeStruct((b, t, D_MODEL), F32),
                         jax.ShapeDtypeStruct((b, ATT_WIDTH, BAND_ROWS), F32),
                         jax.ShapeDtypeStruct((b, ATT_WIDTH, BAND_ROWS), F32),
                         jax.ShapeDtypeStruct((b, t, ATT_WIDTH), BF16),
                         jax.ShapeDtypeStruct((b, t, ATT_WIDTH), BF16)]
        else:
            assert tq == t
            head_spec = pl.BlockSpec((bb, N_ATT_HEADS, t, HEAD_DIM), lambda i, j: (i, 0, 0, 0),
                                     pipeline_mode=pl.Buffered(1))
            kt_spec = pl.BlockSpec((None, ATT_WIDTH, bb * t), lambda i, j: (i, 0, 0))
            out_specs = [tile(D_MODEL), head_spec, head_spec, kt_spec, kt_spec]
            out_shape = [jax.ShapeDtypeStruct((b, t, D_MODEL), F32),
                         jax.ShapeDtypeStruct((b, N_ATT_HEADS, t, HEAD_DIM), F32),
                         jax.ShapeDtypeStruct((b, N_ATT_HEADS, t, HEAD_DIM), F32),
                         jax.ShapeDtypeStruct((b // bb, ATT_WIDTH, bb * t), BF16),
                         jax.ShapeDtypeStruct((b // bb, ATT_WIDTH, bb * t), BF16)]
    kern = functools.partial(_post_kernel, bb=bb, tq=tq, tail=tail)
    return pl.pallas_call(
        kern,
        grid=(b // bb, t // tq),
        in_specs=in_specs,
        out_specs=out_specs,
        out_shape=out_shape,
        scratch_shapes=[pltpu.VMEM((bb * tq, MIX_WIDTH), BF16)],
        compiler_params=_params(2),
        name="post_" + tail,
    )(*args)


def _build_band_bias(tab_ref, bias_s, tq):
    kw = BAND_ROWS + tq
    r_io = lax.broadcasted_iota(jnp.int32, (tq, kw), 0)
    i_io = lax.broadcasted_iota(jnp.int32, (tq, kw), 1)
    off = i_io - jnp.bitwise_and(r_io, -CHUNK)
    in_band = jnp.logical_and(off >= 0, off < BAND_KEYS)
    for h in range(N_ATT_HEADS):
        row = jnp.broadcast_to(tab_ref[h:h + 1, :], (tq, BIAS_TABLE_LEN))
        rolled = pltpu.roll(row, 0, 1, stride=1, stride_axis=0)[:, :kw]
        bias_s[h * tq:(h + 1) * tq, :] = jnp.where(in_band, rolled * LOG2_E, NEG_INF)


def _project_queries(x_ref, g_ref, win_ref, mixq_ref, qst_s, bb, tq, ts):
    rows = bb * tq
    x = x_ref[...].reshape(rows, D_MODEL)
    hn = _rmsnorm(x, g_ref[...]).astype(BF16)
    proj = _dot(hn, win_ref[...])
    mixq_ref[:, :, ATT_WIDTH:] = (proj[:, ATT_WIDTH:] * Q_SCALE).reshape(bb, tq, MEM_WIDTH).astype(BF16)
    lane_head = _lane_head(rows)
    for g in range(N_ATT_GROUPS):
        qg = proj[:, g * MXU_DIM:(g + 1) * MXU_DIM] * Q_SCALE
        for h in range(HEADS_PER_GROUP):
            blk = jnp.where(lane_head == h, qg, 0.0).astype(BF16)
            qst_s[:, g, h * ts:(h + 1) * ts, :] = blk.reshape(rows // ts, ts, MXU_DIM)


def _mixer_b_sample_kernel(x_ref, g_ref, win_ref, ktn_ref, vtn_ref, pkt_ref, pvt_ref, tab_ref,
                           mixq_ref, kt_s, vt_s, bias_s, qst_s, *, bb, tq):
    group_rows = HEADS_PER_GROUP * tq

    @pl.when(pl.program_id(0) == 0)
    def _build_bias():
        _build_band_bias(tab_ref, bias_s, tq)

    kt_s[:, :, :BAND_ROWS] = pkt_ref[...].astype(BF16)
    vt_s[:, :, :BAND_ROWS] = pvt_ref[...].astype(BF16)
    for b in range(bb):
        kt_s[b, :, BAND_ROWS:] = ktn_ref[:, b * tq:(b + 1) * tq]
        vt_s[b, :, BAND_ROWS:] = vtn_ref[:, b * tq:(b + 1) * tq]
    _project_queries(x_ref, g_ref, win_ref, mixq_ref, qst_s, bb, tq, tq)

    for b in range(bb):
        for g in range(N_ATT_GROUPS):
            ls = slice(g * MXU_DIM, (g + 1) * MXU_DIM)
            s = _group_scores(qst_s[b, g], kt_s[b, ls, :], bias_s[g * group_rows:(g + 1) * group_rows, :],
                              k_is_transposed=True)
            e, row_sum = _group_exp(s)
            out = _group_values(e, row_sum, vt_s[b, ls, :], v_is_transposed=True)
            mixq_ref[b, :, ls] = out.astype(BF16)


def _mixer_b_sample(x, g, w_in, kt_new, vt_new, past_kt, past_vt, table, *, bb):
    b, t, _ = x.shape
    tq = t
    nb = kt_new.shape[2] // t
    assert t == CHUNK and b % bb == 0 and nb % bb == 0 and (bb * t) % 128 == 0
    per = nb // bb
    kw = BAND_ROWS + tq
    new_spec = pl.BlockSpec((None, ATT_WIDTH, bb * t), lambda i: (i // per, 0, i % per))
    past_spec = pl.BlockSpec((bb, ATT_WIDTH, BAND_ROWS), lambda i: (i, 0, 0))
    return pl.pallas_call(
        functools.partial(_mixer_b_sample_kernel, bb=bb, tq=tq),
        grid=(b // bb,),
        in_specs=[
            pl.BlockSpec((bb, tq, D_MODEL), lambda i: (i, 0, 0)),
            _const_spec((1, D_MODEL)),
            _const_spec((D_MODEL, ATT_WIDTH + MEM_WIDTH)),
            new_spec, new_spec, past_spec, past_spec,
            _const_spec((N_ATT_HEADS, BIAS_TABLE_LEN)),
        ],
        out_specs=pl.BlockSpec((bb, tq, MIX_WIDTH), lambda i: (i, 0, 0)),
        out_shape=jax.ShapeDtypeStruct((b, t, MIX_WIDTH), BF16),
        scratch_shapes=[
            pltpu.VMEM((bb, ATT_WIDTH, kw), BF16),
            pltpu.VMEM((bb, ATT_WIDTH, kw), BF16),
            pltpu.VMEM((N_ATT_HEADS * tq, kw), F32),
            pltpu.VMEM((bb, N_ATT_GROUPS, HEADS_PER_GROUP * tq, MXU_DIM), BF16),
        ],
        compiler_params=_params(1),
        name="mixer_b_sample",
    )(x, g, w_in, kt_new, vt_new, past_kt, past_vt, table)


def _mixer_b_kernel(x_ref, g_ref, win_ref, kb_ref, vb_ref, tab_ref, mixq_ref, kpad_s, vpad_s, bias_s,
                    qst_s, *, tq, ts):
    bi = pl.program_id(0)
    j = pl.program_id(1)
    kw = BAND_ROWS + ts
    group_rows = HEADS_PER_GROUP * ts

    @pl.when(jnp.logical_and(bi == 0, j == 0))
    def _build_bias():
        _build_band_bias(tab_ref, bias_s, ts)

    @pl.when(j == 0)
    def _fill_kv():
        kpad_s[:, :BAND_ROWS, :] = jnp.zeros((1, BAND_ROWS, ATT_WIDTH), BF16)
        vpad_s[:, :BAND_ROWS, :] = jnp.zeros((1, BAND_ROWS, ATT_WIDTH), BF16)
        kpad_s[:, BAND_ROWS:, :] = kb_ref[...]
        vpad_s[:, BAND_ROWS:, :] = vb_ref[...]

    _project_queries(x_ref, g_ref, win_ref, mixq_ref, qst_s, 1, tq, ts)
    lanes = [slice(g * MXU_DIM, (g + 1) * MXU_DIM) for g in range(N_ATT_GROUPS)]

    def attend(mask_keys):
        units = [(u, g) for u in range(tq // ts) for g in range(N_ATT_GROUPS)]

        def scores(u, g):
            start = pl.multiple_of(j * tq + u * ts, ts)
            key_ok = None
            if mask_keys:
                key_ok = lax.broadcasted_iota(jnp.int32, (1, kw), 1) >= BAND_ROWS - start
            return _group_scores(qst_s[u, g], kpad_s[0, pl.ds(start, kw), lanes[g]],
                                 bias_s[g * group_rows:(g + 1) * group_rows, :], key_ok)

        s_next = scores(*units[0])
        for n, (u, g) in enumerate(units):
            s_cur = s_next
            if n + 1 < len(units):
                s_next = scores(*units[n + 1])
            e, row_sum = _group_exp(s_cur)
            start = pl.multiple_of(j * tq + u * ts, ts)
            out = _group_values(e, row_sum, vpad_s[0, pl.ds(start, kw), lanes[g]])
            mixq_ref[0, u * ts:(u + 1) * ts, lanes[g]] = out.astype(BF16)

    n_masked_tiles = BAND_ROWS // tq
    pl.when(j < n_masked_tiles)(lambda: attend(True))
    pl.when(j >= n_masked_tiles)(lambda: attend(False))


def _mixer_b(x, g, w_in, kb, vb, table, *, tq, ts):
    b, t, _ = x.shape
    assert t % tq == 0 and tq % ts == 0 and ts % CHUNK == 0 and BAND_ROWS % tq == 0
    assert BAND_ROWS + 2 * ts - 1 <= BIAS_TABLE_LEN
    kw = BAND_ROWS + ts
    return pl.pallas_call(
        functools.partial(_mixer_b_kernel, tq=tq, ts=ts),
        grid=(b, t // tq),
        in_specs=[
            pl.BlockSpec((1, tq, D_MODEL), lambda i, j: (i, j, 0)),
            _const_spec((1, D_MODEL)),
            _const_spec((D_MODEL, ATT_WIDTH + MEM_WIDTH)),
            pl.BlockSpec((1, t, ATT_WIDTH), lambda i, j: (i, 0, 0)),
            pl.BlockSpec((1, t, ATT_WIDTH), lambda i, j: (i, 0, 0)),
            _const_spec((N_ATT_HEADS, BIAS_TABLE_LEN)),
        ],
        out_specs=pl.BlockSpec((1, tq, MIX_WIDTH), lambda i, j: (i, j, 0)),
        out_shape=jax.ShapeDtypeStruct((b, t, MIX_WIDTH), BF16),
        scratch_shapes=[
            pltpu.VMEM((1, BAND_ROWS + t, ATT_WIDTH), BF16),
            pltpu.VMEM((1, BAND_ROWS + t, ATT_WIDTH), BF16),
            pltpu.VMEM((N_ATT_HEADS * ts, kw), F32),
            pltpu.VMEM((tq // ts, N_ATT_GROUPS, HEADS_PER_GROUP * ts, MXU_DIM), BF16),
        ],
        compiler_params=_params(2),
        name="mixer_b",
    )(x, g, w_in, kb, vb, table)


def _block_diag_groups(w):
    per = GATE_GROUP // HEAD_DIM
    w4 = w.reshape(N_GATE_GROUPS, per, HEAD_DIM, HEAD_DIM)
    eye = jnp.eye(per, dtype=w.dtype)
    bd = w4[:, :, :, None, :] * eye[None, :, None, :, None]
    return bd.reshape(N_GATE_GROUPS, GATE_GROUP, GATE_GROUP)


def _bias_table(rel):
    top = rel[-1:]
    head = jnp.broadcast_to(top, (BAND_ROWS - REL_CLIP, rel.shape[1]))
    tail = jnp.broadcast_to(top, (BIAS_TABLE_LEN - (BAND_ROWS - REL_CLIP) - rel.shape[0], rel.shape[1]))
    return jnp.concatenate([head, rel[::-1], tail], axis=0).T


def _trunk(x, is_prompt, conv0, h0, past_kt, past_vt, mem_kt, mem_vt, p, *, a_tile, post_tile, b_tile):
    b, t, _ = x.shape
    mixq, conv_s, lru_s = _mixer_a(
        x, p['g_mix'][0], p['w_in_a'], p['conv_w'], p['conv_b'], p['wa_bd'], p['wi_bd'], p['b_rg_a'],
        p['b_rg_i'], p['lam'], jnp.swapaxes(conv0, 0, 1), h0, bb=a_tile[0], tt=a_tile[1],
        reset_first=is_prompt)
    x, k_out, v_out, k_att, v_att = _post(
        x, mixq, mem_kt, mem_vt, 0, p['w_out'], p['g_ffn'], p['w_gu'], p['w_d'], p['g_kv'], p['w_kv'],
        bb=post_tile[0], tq=post_tile[1], tail='kv_prompt' if is_prompt else 'kv_sample')
    if is_prompt:
        mixq = _mixer_b(x, p['g_mix'][1], p['w_in_b'], k_att, v_att, p['bias_table'], tq=b_tile[0],
                        ts=b_tile[1])
        k_new = jnp.transpose(k_out.reshape(b, N_ATT_HEADS, HEAD_DIM, -1), (0, 3, 1, 2))
        v_new = jnp.transpose(v_out.reshape(b, N_ATT_HEADS, HEAD_DIM, -1), (0, 3, 1, 2))
    else:
        mixq = _mixer_b_sample(x, p['g_mix'][1], p['w_in_b'], k_att, v_att, past_kt, past_vt,
                               p['bias_table'], bb=b_tile[0])
        k_new = jnp.transpose(k_out, (0, 2, 1, 3))
        v_new = jnp.transpose(v_out, (0, 2, 1, 3))
    (y,) = _post(
        x, mixq, mem_kt, mem_vt, 1, p['w_out'], p['g_ffn'], p['w_gu'], p['w_d'], p['g_final'],
        bb=post_tile[0], tq=post_tile[1], tail='final')
    return y, jnp.swapaxes(conv_s, 0, 1)[None], lru_s[None], k_new, v_new


def kernel(x_prompt, x_sample, state_conv, state_lru, cache_k, cache_v, cache_mem_k, cache_mem_v, mem_prompt, g_mix, g_ffn, g_final, w_in_a, conv_w, conv_b, w_rg_a, b_rg_a, w_rg_i, b_rg_i, lru_lambda, g_kv, w_kv, w_in_b, rel_bias, g_mem, w_mem_kv, w_out, w_ffn_gu, w_ffn_down):
    depth = g_mix.shape[0]
    row = lambda v: v.reshape(1, -1)
    p = {
        'g_mix': [row(g_mix[l]) for l in range(depth)],
        'g_ffn': g_ffn.reshape(depth, 1, D_MODEL),
        'g_final': row(g_final), 'g_kv': row(g_kv),
        'w_in_a': w_in_a[0].astype(BF16), 'conv_w': conv_w[0], 'conv_b': row(conv_b[0]),
        'wa_bd': _block_diag_groups(w_rg_a[0]).astype(BF16),
        'wi_bd': _block_diag_groups(w_rg_i[0]).astype(BF16),
        'b_rg_a': row(b_rg_a[0]), 'b_rg_i': row(b_rg_i[0]), 'lam': row(lru_lambda[0]),
        'w_kv': w_kv.astype(BF16),
        'w_in_b': w_in_b[0].astype(BF16), 'bias_table': _bias_table(rel_bias[0]),
        'w_out': w_out.astype(BF16), 'w_gu': w_ffn_gu.astype(BF16), 'w_d': w_ffn_down.astype(BF16),
    }
    bp = x_prompt.shape[0]
    bs = x_sample.shape[0]

    def keys_minor(a):
        nd = a.ndim
        a = jnp.transpose(a, tuple(range(nd - 3)) + (nd - 2, nd - 1, nd - 3))
        return a.reshape(a.shape[:nd - 3] + (a.shape[nd - 3] * a.shape[nd - 2], a.shape[nd - 1]))

    def keys_major(a, heads):
        nd = a.ndim
        a = a.reshape(a.shape[:nd - 2] + (heads, HEAD_DIM, a.shape[nd - 1]))
        return jnp.transpose(a, tuple(range(nd - 2)) + (nd, nd - 2, nd - 1))

    mem_kt_p, mem_vt_p = _mem_kv(mem_prompt, g_mem, w_mem_kv.astype(BF16))
    conv0 = jnp.zeros((bp, CONV_WIDTH - 1, LRU_WIDTH), F32)
    lru0 = jnp.zeros((bp, LRU_WIDTH), F32)
    y_p, conv_p, lru_p, k_p, v_p = _trunk(
        x_prompt, True, conv0, lru0, None, None, mem_kt_p, mem_vt_p, p,
        a_tile=(8, 64), post_tile=(1, 512), b_tile=(256, 64))

    y_s, conv_s, lru_s, k_s, v_s = _trunk(
        x_sample, False, state_conv[0], state_lru[0], keys_minor(cache_k), keys_minor(cache_v),
        keys_minor(cache_mem_k), keys_minor(cache_mem_v), p,
        a_tile=(8, 64), post_tile=(8, 64), b_tile=(4, 64))

    return (y_p, y_s, conv_p, lru_p, k_p, v_p, keys_major(mem_kt_p, N_MEM_HEADS),
            keys_major(mem_vt_p, N_MEM_HEADS), conv_s, lru_s, k_s, v_s)
```

```python
import functools

import jax
import jax.numpy as jnp
from jax import lax
from jax.experimental import pallas as pl
from jax.experimental.pallas import tpu as pltpu

F32 = jnp.float32
BF16 = jnp.bfloat16

D_MODEL = 1024
HEAD_DIM = 64
LRU_WIDTH = 768
ATT_WIDTH = 768
N_ATT_HEADS = ATT_WIDTH // HEAD_DIM
MEM_WIDTH = 256
N_MEM_HEADS = MEM_WIDTH // HEAD_DIM
MIX_WIDTH = LRU_WIDTH + MEM_WIDTH
N_MEM = 256
D_FF = 2816
CONV_WIDTH = 4
CHUNK = 64
BAND_ROWS = 512
BAND_KEYS = BAND_ROWS + CHUNK
REL_CLIP = 256
LRU_C = 8.0
RMS_EPS = 1e-6
NEG_INF = -1e30
ATT_SCALE = HEAD_DIM ** -0.5
LOG2_E = 1.4426950408889634
Q_SCALE = ATT_SCALE * LOG2_E

SUBLANES = 8
MXU_DIM = 256
HEADS_PER_GROUP = MXU_DIM // HEAD_DIM
N_ATT_GROUPS = ATT_WIDTH // MXU_DIM
GATE_GROUP = MXU_DIM
N_GATE_GROUPS = LRU_WIDTH // GATE_GROUP
BIAS_TABLE_LEN = 1024
FF_CHUNKS = ((0, 1024), (1024, 2048), (2048, D_FF))
VMEM_LIMIT_BYTES = 58 * 1024 * 1024


def _rmsnorm(x, g):
    return x * lax.rsqrt(jnp.mean(x * x, axis=-1, keepdims=True) + RMS_EPS) * g


def _dot(a, b):
    return jnp.dot(a, b, preferred_element_type=F32)


def _dot_nt(a, b):
    return lax.dot_general(a, b, (((1,), (1,)), ((), ())), preferred_element_type=F32)


def _const_spec(shape):
    nd = len(shape)
    return pl.BlockSpec(shape, lambda *_: (0,) * nd, pipeline_mode=pl.Buffered(1))


def _layer_spec(shape, layer):
    nd = len(shape)
    return pl.BlockSpec((None,) + tuple(shape), lambda *_: (layer,) + (0,) * nd,
                        pipeline_mode=pl.Buffered(1))


def _lane_head(tq):
    return lax.broadcasted_iota(jnp.int32, (tq, MXU_DIM), 1) // HEAD_DIM


def _stack_heads(q):
    lane_head = _lane_head(q.shape[0])
    return jnp.concatenate(
        [jnp.where(lane_head == h, q, 0.0) for h in range(HEADS_PER_GROUP)], axis=0).astype(BF16)


def _group_scores(q_stack, k, bias=None, key_ok=None, *, k_is_transposed=False):
    s = _dot(q_stack, k) if k_is_transposed else _dot_nt(q_stack, k)
    if bias is not None:
        s = s + bias
    if key_ok is not None:
        s = jnp.where(key_ok, s, NEG_INF)
    return s


def _group_exp(s):
    e = jnp.exp2(s - jnp.max(s, axis=-1, keepdims=True))
    return e.astype(BF16), jnp.sum(e, axis=-1, keepdims=True)


def _group_values(e, row_sum, v, *, v_is_transposed=False):
    tq = e.shape[0] // HEADS_PER_GROUP
    o = (_dot_nt(e, v) if v_is_transposed else _dot(e, v)) / row_sum
    lane_head = _lane_head(tq)
    out = jnp.where(lane_head == 0, o[:tq], 0.0)
    for h in range(1, HEADS_PER_GROUP):
        out = out + jnp.where(lane_head == h, o[h * tq:(h + 1) * tq], 0.0)
    return out


def _params(n_axes):
    return pltpu.CompilerParams(dimension_semantics=("arbitrary",) * n_axes,
                                vmem_limit_bytes=VMEM_LIMIT_BYTES)


def _mem_kv_kernel(x_ref, g_ref, w_ref, kt_ref, vt_ref, *, nb):
    h = _rmsnorm(x_ref[...].reshape(nb * N_MEM, D_MODEL), g_ref[...]).astype(BF16)
    kv = _dot(h, w_ref[...])
    for b in range(nb):
        blk = kv[b * N_MEM:(b + 1) * N_MEM, :]
        kt_ref[b] = blk[:, :MEM_WIDTH].T
        vt_ref[b] = blk[:, MEM_WIDTH:].T


def _mem_kv(mem, g_mem, w_mem_kv):
    depth = g_mem.shape[0]
    b = mem.shape[0]
    nb = 4
    assert b % nb == 0
    out = jax.ShapeDtypeStruct((depth, b, MEM_WIDTH, N_MEM), F32)
    return pl.pallas_call(
        functools.partial(_mem_kv_kernel, nb=nb),
        grid=(depth, b // nb),
        in_specs=[
            pl.BlockSpec((nb, N_MEM, D_MODEL), lambda l, i: (i, 0, 0)),
            pl.BlockSpec((None, 1, D_MODEL), lambda l, i: (l, 0, 0)),
            pl.BlockSpec((None, D_MODEL, 2 * MEM_WIDTH), lambda l, i: (l, 0, 0)),
        ],
        out_specs=[
            pl.BlockSpec((None, nb, MEM_WIDTH, N_MEM), lambda l, i: (l, i, 0, 0)),
            pl.BlockSpec((None, nb, MEM_WIDTH, N_MEM), lambda l, i: (l, i, 0, 0)),
        ],
        out_shape=[out, out],
        compiler_params=_params(2),
        name="mem_kv",
    )(mem, g_mem.reshape(depth, 1, D_MODEL), w_mem_kv)


def _softplus(y):
    return jnp.maximum(y, 0.0) + jnp.log1p(jnp.exp(-jnp.abs(y)))


def _gelu_tanh(x):
    c = 0.7978845608028654
    return x * (0.5 * (1.0 + jnp.tanh(c * (x + 0.044715 * (x * x * x)))))


def _mixer_a_kernel(x_ref, perm_ref, perm_t_ref, g_ref, win_ref, cw_ref, cb_ref, wa_ref, wi_ref,
                    ba_ref, bi_ref, lam_ref, conv0_ref, h0_ref,
                    mixq_ref, convs_ref, lrus_ref,
                    xpad_s, h_s, carry_s, *, bb, tt, reset_first):
    t_idx = pl.program_id(1)
    rows = bb * tt
    tsub = MXU_DIM // bb
    hist = (CONV_WIDTH - 1) * bb

    @pl.when(t_idx == 0)
    def _init():
        xpad_s[0:hist, :] = conv0_ref[...].reshape(hist, LRU_WIDTH)
        carry_s[...] = h0_ref[...]

    sp_neg_lam = _softplus(-lam_ref[...])

    def project(s):
        xs = x_ref[:, s * tsub:(s + 1) * tsub, :].reshape(MXU_DIM, D_MODEL)
        hn = _rmsnorm(xs, g_ref[...]).astype(BF16)
        return _dot(_dot(perm_ref[...], hn).astype(BF16), win_ref[...])

    def conv_and_gates(s, proj):
        lo = s * MXU_DIM
        xpad_s[hist + lo:hist + lo + MXU_DIM, :] = proj[:, :LRU_WIDTH]
        xc = cb_ref[...] + xpad_s[hist + lo:hist + lo + MXU_DIM, :] * cw_ref[CONV_WIDTH - 1:CONV_WIDTH, :]
        for k in range(CONV_WIDTH - 1):
            xc = xc + xpad_s[k * bb + lo:k * bb + lo + MXU_DIM, :] * cw_ref[k:k + 1, :]
        xcb = xc.astype(BF16)
        r_parts, i_parts = [], []
        for gi in range(N_GATE_GROUPS):
            sl = slice(gi * GATE_GROUP, (gi + 1) * GATE_GROUP)
            r_parts.append(_dot(xcb[:, sl], wa_ref[gi]))
            i_parts.append(_dot(xcb[:, sl], wi_ref[gi]))
        return xc, jnp.concatenate(r_parts, axis=-1), jnp.concatenate(i_parts, axis=-1)

    def recur_and_emit(s, proj, xc, r_pre, i_pre, h):
        r = jax.nn.sigmoid(r_pre + ba_ref[...])
        ig = jax.nn.sigmoid(i_pre + bi_ref[...])
        log_a = (-LRU_C * r) * sp_neg_lam
        a = jnp.exp(log_a)
        th = jnp.tanh(log_a)
        z = -2.0 * th / (1.0 - th)
        mult = jnp.where(z > 0.0, z * lax.rsqrt(z), 0.0)
        if reset_first and s == 0:
            row = lax.broadcasted_iota(jnp.int32, (MXU_DIM, 1), 0)
            mult = jnp.where(jnp.logical_and(row < bb, t_idx == 0), 1.0, mult)
        u = mult * (ig * xc)
        lo = s * MXU_DIM
        for t in range(tsub):
            h = a[t * bb:(t + 1) * bb, :] * h + u[t * bb:(t + 1) * bb, :]
            h_s[lo + t * bb:lo + (t + 1) * bb, :] = h
        mix = (h_s[lo:lo + MXU_DIM, :] * _gelu_tanh(proj[:, LRU_WIDTH:2 * LRU_WIDTH])).astype(BF16)
        mixq_tm = jnp.concatenate([mix, (proj[:, 2 * LRU_WIDTH:] * Q_SCALE).astype(BF16)], axis=-1)
        blk = _dot(perm_t_ref[...], mixq_tm).astype(BF16)
        mixq_ref[:, s * tsub:(s + 1) * tsub, :] = blk.reshape(bb, tsub, MIX_WIDTH)
        return h

    n_sub = tt // tsub
    h = carry_s[...]
    proj_next = project(0)
    for s in range(n_sub):
        proj = proj_next
        if s + 1 < n_sub:
            proj_next = project(s + 1)
        xc, r_pre, i_pre = conv_and_gates(s, proj)
        h = recur_and_emit(s, proj, xc, r_pre, i_pre, h)
    carry_s[...] = h
    lrus_ref[...] = h
    tail = xpad_s[rows:rows + hist, :]
    convs_ref[...] = tail.reshape(CONV_WIDTH - 1, bb, LRU_WIDTH)
    xpad_s[0:hist, :] = tail


def _mixer_a(x, g, w_in, conv_w, conv_b, wa_bd, wi_bd, b_a, b_i, lam, conv0_tm, h0, *, bb, tt,
             reset_first):
    b, t, _ = x.shape
    assert bb == SUBLANES and b % bb == 0 and t % tt == 0 and MXU_DIM % bb == 0
    tsub = MXU_DIM // bb
    assert tt % tsub == 0
    rows = bb * tt
    tm = jnp.arange(MXU_DIM)
    perm = (((tm % bb) * tsub + tm // bb)[:, None] == jnp.arange(MXU_DIM)[None, :]).astype(BF16)
    kern = functools.partial(_mixer_a_kernel, bb=bb, tt=tt, reset_first=reset_first)
    return pl.pallas_call(
        kern,
        grid=(b // bb, t // tt),
        in_specs=[
            pl.BlockSpec((bb, tt, D_MODEL), lambda i, j: (i, j, 0)),
            _const_spec((MXU_DIM, MXU_DIM)),
            _const_spec((MXU_DIM, MXU_DIM)),
            _const_spec((1, D_MODEL)),
            _const_spec((D_MODEL, 2 * LRU_WIDTH + MEM_WIDTH)),
            _const_spec((CONV_WIDTH, LRU_WIDTH)),
            _const_spec((1, LRU_WIDTH)),
            _const_spec((N_GATE_GROUPS, GATE_GROUP, GATE_GROUP)),
            _const_spec((N_GATE_GROUPS, GATE_GROUP, GATE_GROUP)),
            _const_spec((1, LRU_WIDTH)),
            _const_spec((1, LRU_WIDTH)),
            _const_spec((1, LRU_WIDTH)),
            pl.BlockSpec((CONV_WIDTH - 1, bb, LRU_WIDTH), lambda i, j: (0, i, 0)),
            pl.BlockSpec((bb, LRU_WIDTH), lambda i, j: (i, 0)),
        ],
        out_specs=[
            pl.BlockSpec((bb, tt, MIX_WIDTH), lambda i, j: (i, j, 0)),
            pl.BlockSpec((CONV_WIDTH - 1, bb, LRU_WIDTH), lambda i, j: (0, i, 0)),
            pl.BlockSpec((bb, LRU_WIDTH), lambda i, j: (i, 0)),
        ],
        out_shape=[
            jax.ShapeDtypeStruct((b, t, MIX_WIDTH), BF16),
            jax.ShapeDtypeStruct((CONV_WIDTH - 1, b, LRU_WIDTH), F32),
            jax.ShapeDtypeStruct((b, LRU_WIDTH), F32),
        ],
        scratch_shapes=[
            pltpu.VMEM(((CONV_WIDTH - 1) * bb + rows, LRU_WIDTH), F32),
            pltpu.VMEM((rows, LRU_WIDTH), F32),
            pltpu.VMEM((bb, LRU_WIDTH), F32),
        ],
        compiler_params=_params(2),
        name="mixer_a",
    )(x, perm, perm.T, g, w_in, conv_w, conv_b, wa_bd, wi_bd, b_a, b_i, lam, conv0_tm, h0)


def _post_kernel(x_ref, mixq_ref, mkt_ref, mvt_ref, wo_ref, gf_ref, wgu_ref, wd_ref, gt_ref, *rest,
                 bb, tq, tail):
    if tail == 'final':
        y_ref, cat_s = rest
    else:
        wkv_ref, x2_ref, k_out_ref, v_out_ref, k_att_ref, v_att_ref, cat_s = rest
    rows = bb * tq
    mix_w = MIX_WIDTH - MEM_WIDTH

    cat_s[:, :mix_w] = mixq_ref[:, :, :mix_w].reshape(rows, mix_w)
    for b in range(bb):
        q = mixq_ref[b, :, mix_w:].astype(F32)
        s = _group_scores(_stack_heads(q), mkt_ref[b].astype(BF16), k_is_transposed=True)
        e, row_sum = _group_exp(s)
        o = _group_values(e, row_sum, mvt_ref[b].astype(BF16), v_is_transposed=True)
        cat_s[b * tq:(b + 1) * tq, mix_w:] = o.astype(BF16)

    x = x_ref[...].reshape(rows, D_MODEL)
    x1 = x + _dot(cat_s[...], wo_ref[...])
    hf = _rmsnorm(x1, gf_ref[...]).astype(BF16)
    acts = []
    for c0, c1 in FF_CHUNKS:
        gg = _dot(hf, wgu_ref[:, c0:c1])
        uu = _dot(hf, wgu_ref[:, D_FF + c0:D_FF + c1])
        acts.append(((gg * jax.nn.sigmoid(gg)) * uu).astype(BF16))
    x2 = x1 + _dot(jnp.concatenate(acts, axis=-1), wd_ref[...])

    if tail == 'final':
        y_ref[...] = _rmsnorm(x2, gt_ref[...]).reshape(bb, tq, D_MODEL)
        return
    x2_ref[...] = x2.reshape(bb, tq, D_MODEL)
    hk = _rmsnorm(x2, gt_ref[...]).astype(BF16)
    kv = _dot(hk, wkv_ref[...])
    k = kv[:, :ATT_WIDTH]
    v = kv[:, ATT_WIDTH:]
    if tail == 'kv_prompt':
        k_att_ref[...] = k.reshape(bb, tq, ATT_WIDTH).astype(BF16)
        v_att_ref[...] = v.reshape(bb, tq, ATT_WIDTH).astype(BF16)

        @pl.when(pl.program_id(1) == pl.num_programs(1) - 1)
        def _write_cache():
            k_out_ref[0] = k.T
            v_out_ref[0] = v.T
    else:
        for b in range(bb):
            for h in range(N_ATT_HEADS):
                hs = slice(h * HEAD_DIM, (h + 1) * HEAD_DIM)
                k_out_ref[b, h] = k[b * tq:(b + 1) * tq, hs]
                v_out_ref[b, h] = v[b * tq:(b + 1) * tq, hs]
        k_att_ref[...] = k.T.astype(BF16)
        v_att_ref[...] = v.T.astype(BF16)


def _post(x, mixq, mem_kt, mem_vt, layer, w_out, g_ffn, w_gu, w_d, g_tail, w_kv=None, *, bb, tq, tail):
    b, t, _ = x.shape
    assert b % bb == 0 and t % tq == 0
    tile = lambda w: pl.BlockSpec((bb, tq, w), lambda i, j: (i, j, 0))
    mem_spec = pl.BlockSpec((None, bb, MEM_WIDTH, N_MEM), lambda i, j: (layer, i, 0, 0))
    in_specs = [
        tile(D_MODEL), tile(MIX_WIDTH), mem_spec, mem_spec,
        _layer_spec((MIX_WIDTH, D_MODEL), layer),
        _layer_spec((1, D_MODEL), layer),
        _layer_spec((D_MODEL, 2 * D_FF), layer),
        _layer_spec((D_FF, D_MODEL), layer),
        _const_spec((1, D_MODEL)),
    ]
    args = [x, mixq, mem_kt, mem_vt, w_out, g_ffn, w_gu, w_d, g_tail]
    if tail == 'final':
        out_specs = [tile(D_MODEL)]
        out_shape = [jax.ShapeDtypeStruct((b, t, D_MODEL), F32)]
    else:
        in_specs.append(_const_spec((D_MODEL, 2 * ATT_WIDTH)))
        args.append(w_kv)
        if tail == 'kv_prompt':
            assert bb == 1 and tq == BAND_ROWS
            cache_spec = pl.BlockSpec((1, ATT_WIDTH, BAND_ROWS), lambda i, j: (i, 0, 0))
            out_specs = [tile(D_MODEL), cache_spec, cache_spec, tile(ATT_WIDTH), tile(ATT_WIDTH)]
            out_shape = [jax.ShapeDtypeStruct((b, t, D_MODEL), F32),
                         jax.ShapeDtypeStruct((b, ATT_WIDTH, BAND_ROWS), F32),
                         jax.ShapeDtypeStruct((b, ATT_WIDTH, BAND_ROWS), F32),
                         jax.ShapeDtypeStruct((b, t, ATT_WIDTH), BF16),
                         jax.ShapeDtypeStruct((b, t, ATT_WIDTH), BF16)]
        else:
            assert tq == t
            head_spec = pl.BlockSpec((bb, N_ATT_HEADS, t, HEAD_DIM), lambda i, j: (i, 0, 0, 0),
                                     pipeline_mode=pl.Buffered(1))
            kt_spec = pl.BlockSpec((None, ATT_WIDTH, bb * t), lambda i, j: (i, 0, 0))
            out_specs = [tile(D_MODEL), head_spec, head_spec, kt_spec, kt_spec]
            out_shape = [jax.ShapeDtypeStruct((b, t, D_MODEL), F32),
                         jax.ShapeDtypeStruct((b, N_ATT_HEADS, t, HEAD_DIM), F32),
                         jax.ShapeDtypeStruct((b, N_ATT_HEADS, t, HEAD_DIM), F32),
                         jax.ShapeDtypeStruct((b // bb, ATT_WIDTH, bb * t), BF16),
                         jax.ShapeDtypeStruct((b // bb, ATT_WIDTH, bb * t), BF16)]
    kern = functools.partial(_post_kernel, bb=bb, tq=tq, tail=tail)
    return pl.pallas_call(
        kern,
        grid=(b // bb, t // tq),
        in_specs=in_specs,
        out_specs=out_specs,
        out_shape=out_shape,
        scratch_shapes=[pltpu.VMEM((bb * tq, MIX_WIDTH), BF16)],
        compiler_params=_params(2),
        name="post_" + tail,
    )(*args)


def _build_band_bias(tab_ref, bias_s, tq):
    kw = BAND_ROWS + tq
    r_io = lax.broadcasted_iota(jnp.int32, (tq, kw), 0)
    i_io = lax.broadcasted_iota(jnp.int32, (tq, kw), 1)
    off = i_io - jnp.bitwise_and(r_io, -CHUNK)
    in_band = jnp.logical_and(off >= 0, off < BAND_KEYS)
    for h in range(N_ATT_HEADS):
        row = jnp.broadcast_to(tab_ref[h:h + 1, :], (tq, BIAS_TABLE_LEN))
        rolled = pltpu.roll(row, 0, 1, stride=1, stride_axis=0)[:, :kw]
        bias_s[h * tq:(h + 1) * tq, :] = jnp.where(in_band, rolled * LOG2_E, NEG_INF)


def _project_queries(x_ref, g_ref, win_ref, mixq_ref, qst_s, bb, tq, ts):
    rows = bb * tq
    x = x_ref[...].reshape(rows, D_MODEL)
    hn = _rmsnorm(x, g_ref[...]).astype(BF16)
    proj = _dot(hn, win_ref[...])
    mixq_ref[:, :, ATT_WIDTH:] = (proj[:, ATT_WIDTH:] * Q_SCALE).reshape(bb, tq, MEM_WIDTH).astype(BF16)
    lane_head = _lane_head(rows)
    for g in range(N_ATT_GROUPS):
        qg = proj[:, g * MXU_DIM:(g + 1) * MXU_DIM] * Q_SCALE
        for h in range(HEADS_PER_GROUP):
            blk = jnp.where(lane_head == h, qg, 0.0).astype(BF16)
            qst_s[:, g, h * ts:(h + 1) * ts, :] = blk.reshape(rows // ts, ts, MXU_DIM)


def _mixer_b_sample_kernel(x_ref, g_ref, win_ref, ktn_ref, vtn_ref, pkt_ref, pvt_ref, tab_ref,
                           mixq_ref, kt_s, vt_s, bias_s, qst_s, *, bb, tq):
    group_rows = HEADS_PER_GROUP * tq

    @pl.when(pl.program_id(0) == 0)
    def _build_bias():
        _build_band_bias(tab_ref, bias_s, tq)

    kt_s[:, :, :BAND_ROWS] = pkt_ref[...].astype(BF16)
    vt_s[:, :, :BAND_ROWS] = pvt_ref[...].astype(BF16)
    for b in range(bb):
        kt_s[b, :, BAND_ROWS:] = ktn_ref[:, b * tq:(b + 1) * tq]
        vt_s[b, :, BAND_ROWS:] = vtn_ref[:, b * tq:(b + 1) * tq]
    _project_queries(x_ref, g_ref, win_ref, mixq_ref, qst_s, bb, tq, tq)

    for b in range(bb):
        for g in range(N_ATT_GROUPS):
            ls = slice(g * MXU_DIM, (g + 1) * MXU_DIM)
            s = _group_scores(qst_s[b, g], kt_s[b, ls, :], bias_s[g * group_rows:(g + 1) * group_rows, :],
                              k_is_transposed=True)
            e, row_sum = _group_exp(s)
            out = _group_values(e, row_sum, vt_s[b, ls, :], v_is_transposed=True)
            mixq_ref[b, :, ls] = out.astype(BF16)


def _mixer_b_sample(x, g, w_in, kt_new, vt_new, past_kt, past_vt, table, *, bb):
    b, t, _ = x.shape
    tq = t
    nb = kt_new.shape[2] // t
    assert t == CHUNK and b % bb == 0 and nb % bb == 0 and (bb * t) % 128 == 0
    per = nb // bb
    kw = BAND_ROWS + tq
    new_spec = pl.BlockSpec((None, ATT_WIDTH, bb * t), lambda i: (i // per, 0, i % per))
    past_spec = pl.BlockSpec((bb, ATT_WIDTH, BAND_ROWS), lambda i: (i, 0, 0))
    return pl.pallas_call(
        functools.partial(_mixer_b_sample_kernel, bb=bb, tq=tq),
        grid=(b // bb,),
        in_specs=[
            pl.BlockSpec((bb, tq, D_MODEL), lambda i: (i, 0, 0)),
            _const_spec((1, D_MODEL)),
            _const_spec((D_MODEL, ATT_WIDTH + MEM_WIDTH)),
            new_spec, new_spec, past_spec, past_spec,
            _const_spec((N_ATT_HEADS, BIAS_TABLE_LEN)),
        ],
        out_specs=pl.BlockSpec((bb, tq, MIX_WIDTH), lambda i: (i, 0, 0)),
        out_shape=jax.ShapeDtypeStruct((b, t, MIX_WIDTH), BF16),
        scratch_shapes=[
            pltpu.VMEM((bb, ATT_WIDTH, kw), BF16),
            pltpu.VMEM((bb, ATT_WIDTH, kw), BF16),
            pltpu.VMEM((N_ATT_HEADS * tq, kw), F32),
            pltpu.VMEM((bb, N_ATT_GROUPS, HEADS_PER_GROUP * tq, MXU_DIM), BF16),
        ],
        compiler_params=_params(1),
        name="mixer_b_sample",
    )(x, g, w_in, kt_new, vt_new, past_kt, past_vt, table)


def _mixer_b_kernel(x_ref, g_ref, win_ref, kb_ref, vb_ref, tab_ref, mixq_ref, kpad_s, vpad_s, bias_s,
                    qst_s, *, tq, ts):
    bi = pl.program_id(0)
    j = pl.program_id(1)
    kw = BAND_ROWS + ts
    group_rows = HEADS_PER_GROUP * ts

    @pl.when(jnp.logical_and(bi == 0, j == 0))
    def _build_bias():
        _build_band_bias(tab_ref, bias_s, ts)

    @pl.when(j == 0)
    def _fill_kv():
        kpad_s[:, :BAND_ROWS, :] = jnp.zeros((1, BAND_ROWS, ATT_WIDTH), BF16)
        vpad_s[:, :BAND_ROWS, :] = jnp.zeros((1, BAND_ROWS, ATT_WIDTH), BF16)
        kpad_s[:, BAND_ROWS:, :] = kb_ref[...]
        vpad_s[:, BAND_ROWS:, :] = vb_ref[...]

    _project_queries(x_ref, g_ref, win_ref, mixq_ref, qst_s, 1, tq, ts)
    lanes = [slice(g * MXU_DIM, (g + 1) * MXU_DIM) for g in range(N_ATT_GROUPS)]

    def attend(mask_keys):
        units = [(u, g) for u in range(tq // ts) for g in range(N_ATT_GROUPS)]

        def scores(u, g):
            start = pl.multiple_of(j * tq + u * ts, ts)
            key_ok = None
            if mask_keys:
                key_ok = lax.broadcasted_iota(jnp.int32, (1, kw), 1) >= BAND_ROWS - start
            return _group_scores(qst_s[u, g], kpad_s[0, pl.ds(start, kw), lanes[g]],
                                 bias_s[g * group_rows:(g + 1) * group_rows, :], key_ok)

        s_next = scores(*units[0])
        for n, (u, g) in enumerate(units):
            s_cur = s_next
            if n + 1 < len(units):
                s_next = scores(*units[n + 1])
            e, row_sum = _group_exp(s_cur)
            start = pl.multiple_of(j * tq + u * ts, ts)
            out = _group_values(e, row_sum, vpad_s[0, pl.ds(start, kw), lanes[g]])
            mixq_ref[0, u * ts:(u + 1) * ts, lanes[g]] = out.astype(BF16)

    n_masked_tiles = BAND_ROWS // tq
    pl.when(j < n_masked_tiles)(lambda: attend(True))
    pl.when(j >= n_masked_tiles)(lambda: attend(False))


def _mixer_b(x, g, w_in, kb, vb, table, *, tq, ts):
    b, t, _ = x.shape
    assert t % tq == 0 and tq % ts == 0 and ts % CHUNK == 0 and BAND_ROWS % tq == 0
    assert BAND_ROWS + 2 * ts - 1 <= BIAS_TABLE_LEN
    kw = BAND_ROWS + ts
    return pl.pallas_call(
        functools.partial(_mixer_b_kernel, tq=tq, ts=ts),
        grid=(b, t // tq),
        in_specs=[
            pl.BlockSpec((1, tq, D_MODEL), lambda i, j: (i, j, 0)),
            _const_spec((1, D_MODEL)),
            _const_spec((D_MODEL, ATT_WIDTH + MEM_WIDTH)),
            pl.BlockSpec((1, t, ATT_WIDTH), lambda i, j: (i, 0, 0)),
            pl.BlockSpec((1, t, ATT_WIDTH), lambda i, j: (i, 0, 0)),
            _const_spec((N_ATT_HEADS, BIAS_TABLE_LEN)),
        ],
        out_specs=pl.BlockSpec((1, tq, MIX_WIDTH), lambda i, j: (i, j, 0)),
        out_shape=jax.ShapeDtypeStruct((b, t, MIX_WIDTH), BF16),
        scratch_shapes=[
            pltpu.VMEM((1, BAND_ROWS + t, ATT_WIDTH), BF16),
            pltpu.VMEM((1, BAND_ROWS + t, ATT_WIDTH), BF16),
            pltpu.VMEM((N_ATT_HEADS * ts, kw), F32),
            pltpu.VMEM((tq // ts, N_ATT_GROUPS, HEADS_PER_GROUP * ts, MXU_DIM), BF16),
        ],
        compiler_params=_params(2),
        name="mixer_b",
    )(x, g, w_in, kb, vb, table)


def _block_diag_groups(w):
    per = GATE_GROUP // HEAD_DIM
    w4 = w.reshape(N_GATE_GROUPS, per, HEAD_DIM, HEAD_DIM)
    eye = jnp.eye(per, dtype=w.dtype)
    bd = w4[:, :, :, None, :] * eye[None, :, None, :, None]
    return bd.reshape(N_GATE_GROUPS, GATE_GROUP, GATE_GROUP)


def _bias_table(rel):
    top = rel[-1:]
    head = jnp.broadcast_to(top, (BAND_ROWS - REL_CLIP, rel.shape[1]))
    tail = jnp.broadcast_to(top, (BIAS_TABLE_LEN - (BAND_ROWS - REL_CLIP) - rel.shape[0], rel.shape[1]))
    return jnp.concatenate([head, rel[::-1], tail], axis=0).T


def _trunk(x, is_prompt, conv0, h0, past_kt, past_vt, mem_kt, mem_vt, p, *, a_tile, post_tile, b_tile):
    b, t, _ = x.shape
    mixq, conv_s, lru_s = _mixer_a(
        x, p['g_mix'][0], p['w_in_a'], p['conv_w'], p['conv_b'], p['wa_bd'], p['wi_bd'], p['b_rg_a'],
        p['b_rg_i'], p['lam'], jnp.swapaxes(conv0, 0, 1), h0, bb=a_tile[0], tt=a_tile[1],
        reset_first=is_prompt)
    x, k_out, v_out, k_att, v_att = _post(
        x, mixq, mem_kt, mem_vt, 0, p['w_out'], p['g_ffn'], p['w_gu'], p['w_d'], p['g_kv'], p['w_kv'],
        bb=post_tile[0], tq=post_tile[1], tail='kv_prompt' if is_prompt else 'kv_sample')
    if is_prompt:
        mixq = _mixer_b(x, p['g_mix'][1], p['w_in_b'], k_att, v_att, p['bias_table'], tq=b_tile[0],
                        ts=b_tile[1])
        k_new = jnp.transpose(k_out.reshape(b, N_ATT_HEADS, HEAD_DIM, -1), (0, 3, 1, 2))
        v_new = jnp.transpose(v_out.reshape(b, N_ATT_HEADS, HEAD_DIM, -1), (0, 3, 1, 2))
    else:
        mixq = _mixer_b_sample(x, p['g_mix'][1], p['w_in_b'], k_att, v_att, past_kt, past_vt,
                               p['bias_table'], bb=b_tile[0])
        k_new = jnp.transpose(k_out, (0, 2, 1, 3))
        v_new = jnp.transpose(v_out, (0, 2, 1, 3))
    (y,) = _post(
        x, mixq, mem_kt, mem_vt, 1, p['w_out'], p['g_ffn'], p['w_gu'], p['w_d'], p['g_final'],
        bb=post_tile[0], tq=post_tile[1], tail='final')
    return y, jnp.swapaxes(conv_s, 0, 1)[None], lru_s[None], k_new, v_new


def kernel(x_prompt, x_sample, state_conv, state_lru, cache_k, cache_v, cache_mem_k, cache_mem_v, mem_prompt, g_mix, g_ffn, g_final, w_in_a, conv_w, conv_b, w_rg_a, b_rg_a, w_rg_i, b_rg_i, lru_lambda, g_kv, w_kv, w_in_b, rel_bias, g_mem, w_mem_kv, w_out, w_ffn_gu, w_ffn_down):
    depth = g_mix.shape[0]
    row = lambda v: v.reshape(1, -1)
    p = {
        'g_mix': [row(g_mix[l]) for l in range(depth)],
        'g_ffn': g_ffn.reshape(depth, 1, D_MODEL),
        'g_final': row(g_final), 'g_kv': row(g_kv),
        'w_in_a': w_in_a[0].astype(BF16), 'conv_w': conv_w[0], 'conv_b': row(conv_b[0]),
        'wa_bd': _block_diag_groups(w_rg_a[0]).astype(BF16),
        'wi_bd': _block_diag_groups(w_rg_i[0]).astype(BF16),
        'b_rg_a': row(b_rg_a[0]), 'b_rg_i': row(b_rg_i[0]), 'lam': row(lru_lambda[0]),
        'w_kv': w_kv.astype(BF16),
        'w_in_b': w_in_b[0].astype(BF16), 'bias_table': _bias_table(rel_bias[0]),
        'w_out': w_out.astype(BF16), 'w_gu': w_ffn_gu.astype(BF16), 'w_d': w_ffn_down.astype(BF16),
    }
    bp = x_prompt.shape[0]
    bs = x_sample.shape[0]

    def keys_minor(a):
        nd = a.ndim
        a = jnp.transpose(a, tuple(range(nd - 3)) + (nd - 2, nd - 1, nd - 3))
        return a.reshape(a.shape[:nd - 3] + (a.shape[nd - 3] * a.shape[nd - 2], a.shape[nd - 1]))

    def keys_major(a, heads):
        nd = a.ndim
        a = a.reshape(a.shape[:nd - 2] + (heads, HEAD_DIM, a.shape[nd - 1]))
        return jnp.transpose(a, tuple(range(nd - 2)) + (nd, nd - 2, nd - 1))

    mem_kt_p, mem_vt_p = _mem_kv(mem_prompt, g_mem, w_mem_kv.astype(BF16))
    conv0 = jnp.zeros((bp, CONV_WIDTH - 1, LRU_WIDTH), F32)
    lru0 = jnp.zeros((bp, LRU_WIDTH), F32)
    y_p, conv_p, lru_p, k_p, v_p = _trunk(
        x_prompt, True, conv0, lru0, None, None, mem_kt_p, mem_vt_p, p,
        a_tile=(8, 128), post_tile=(1, 512), b_tile=(512, 64))

    y_s, conv_s, lru_s, k_s, v_s = _trunk(
        x_sample, False, state_conv[0], state_lru[0], keys_minor(cache_k), keys_minor(cache_v),
        keys_minor(cache_mem_k), keys_minor(cache_mem_v), p,
        a_tile=(8, 64), post_tile=(8, 64), b_tile=(4, 64))

    return (y_p, y_s, conv_p, lru_p, k_p, v_p, keys_major(mem_kt_p, N_MEM_HEADS),
            keys_major(mem_vt_p, N_MEM_HEADS), conv_s, lru_s, k_s, v_s)
```

```python
import functools

import jax
import jax.numpy as jnp
from jax import lax
from jax.experimental import pallas as pl
from jax.experimental.pallas import tpu as pltpu

F32 = jnp.float32
BF16 = jnp.bfloat16

D_MODEL = 1024
HEAD_DIM = 64
LRU_WIDTH = 768
ATT_WIDTH = 768
N_ATT_HEADS = ATT_WIDTH // HEAD_DIM
MEM_WIDTH = 256
N_MEM_HEADS = MEM_WIDTH // HEAD_DIM
MIX_WIDTH = LRU_WIDTH + MEM_WIDTH
N_MEM = 256
D_FF = 2816
CONV_WIDTH = 4
CHUNK = 64
BAND_ROWS = 512
BAND_KEYS = BAND_ROWS + CHUNK
REL_CLIP = 256
LRU_C = 8.0
RMS_EPS = 1e-6
NEG_INF = -1e30
ATT_SCALE = HEAD_DIM ** -0.5
LOG2_E = 1.4426950408889634
Q_SCALE = ATT_SCALE * LOG2_E

SUBLANES = 8
MXU_DIM = 256
HEADS_PER_GROUP = MXU_DIM // HEAD_DIM
N_ATT_GROUPS = ATT_WIDTH // MXU_DIM
GATE_GROUP = MXU_DIM
N_GATE_GROUPS = LRU_WIDTH // GATE_GROUP
BIAS_TABLE_LEN = 1024
FF_CHUNKS = ((0, 1024), (1024, 2048), (2048, D_FF))
VMEM_LIMIT_BYTES = 58 * 1024 * 1024


def _rmsnorm(x, g):
    return x * lax.rsqrt(jnp.mean(x * x, axis=-1, keepdims=True) + RMS_EPS) * g


def _dot(a, b):
    return jnp.dot(a, b, preferred_element_type=F32)


def _dot_nt(a, b):
    return lax.dot_general(a, b, (((1,), (1,)), ((), ())), preferred_element_type=F32)


def _const_spec(shape):
    nd = len(shape)
    return pl.BlockSpec(shape, lambda *_: (0,) * nd, pipeline_mode=pl.Buffered(1))


def _layer_spec(shape, layer):
    nd = len(shape)
    return pl.BlockSpec((None,) + tuple(shape), lambda *_: (layer,) + (0,) * nd,
                        pipeline_mode=pl.Buffered(1))


def _lane_head(tq):
    return lax.broadcasted_iota(jnp.int32, (tq, MXU_DIM), 1) // HEAD_DIM


def _stack_heads(q):
    lane_head = _lane_head(q.shape[0])
    return jnp.concatenate(
        [jnp.where(lane_head == h, q, 0.0) for h in range(HEADS_PER_GROUP)], axis=0).astype(BF16)


def _group_scores(q_stack, k, bias=None, key_ok=None, *, k_is_transposed=False):
    s = _dot(q_stack, k) if k_is_transposed else _dot_nt(q_stack, k)
    if bias is not None:
        s = s + bias
    if key_ok is not None:
        s = jnp.where(key_ok, s, NEG_INF)
    return s


def _group_exp(s):
    e = jnp.exp2(s - jnp.max(s, axis=-1, keepdims=True))
    return e.astype(BF16), jnp.sum(e, axis=-1, keepdims=True)


def _group_values(e, row_sum, v, *, v_is_transposed=False):
    tq = e.shape[0] // HEADS_PER_GROUP
    o = (_dot_nt(e, v) if v_is_transposed else _dot(e, v)) / row_sum
    lane_head = _lane_head(tq)
    out = jnp.where(lane_head == 0, o[:tq], 0.0)
    for h in range(1, HEADS_PER_GROUP):
        out = out + jnp.where(lane_head == h, o[h * tq:(h + 1) * tq], 0.0)
    return out


def _params(n_axes):
    return pltpu.CompilerParams(dimension_semantics=("arbitrary",) * n_axes,
                                vmem_limit_bytes=VMEM_LIMIT_BYTES)


def _mem_kv_kernel(x_ref, g_ref, w_ref, kt_ref, vt_ref, *, nb):
    h = _rmsnorm(x_ref[...].reshape(nb * N_MEM, D_MODEL), g_ref[...]).astype(BF16)
    kv = _dot(h, w_ref[...])
    for b in range(nb):
        blk = kv[b * N_MEM:(b + 1) * N_MEM, :]
        kt_ref[b] = blk[:, :MEM_WIDTH].T
        vt_ref[b] = blk[:, MEM_WIDTH:].T


def _mem_kv(mem, g_mem, w_mem_kv):
    depth = g_mem.shape[0]
    b = mem.shape[0]
    nb = 4
    assert b % nb == 0
    out = jax.ShapeDtypeStruct((depth, b, MEM_WIDTH, N_MEM), F32)
    return pl.pallas_call(
        functools.partial(_mem_kv_kernel, nb=nb),
        grid=(depth, b // nb),
        in_specs=[
            pl.BlockSpec((nb, N_MEM, D_MODEL), lambda l, i: (i, 0, 0)),
            pl.BlockSpec((None, 1, D_MODEL), lambda l, i: (l, 0, 0)),
            pl.BlockSpec((None, D_MODEL, 2 * MEM_WIDTH), lambda l, i: (l, 0, 0)),
        ],
        out_specs=[
            pl.BlockSpec((None, nb, MEM_WIDTH, N_MEM), lambda l, i: (l, i, 0, 0)),
            pl.BlockSpec((None, nb, MEM_WIDTH, N_MEM), lambda l, i: (l, i, 0, 0)),
        ],
        out_shape=[out, out],
        compiler_params=_params(2),
        name="mem_kv",
    )(mem, g_mem.reshape(depth, 1, D_MODEL), w_mem_kv)


def _softplus(y):
    return jnp.maximum(y, 0.0) + jnp.log1p(jnp.exp(-jnp.abs(y)))


def _gelu_tanh(x):
    c = 0.7978845608028654
    return x * (0.5 * (1.0 + jnp.tanh(c * (x + 0.044715 * (x * x * x)))))


N_MIXER_A_INPUTS = 14


def _mixer_a_kernel(*refs, bb, tt, reset_first, n_cast):
    (x_ref, perm_ref, perm_t_ref, g_ref, win_ref, cw_ref, cb_ref, wa_ref, wi_ref, ba_ref, bi_ref, lam_ref,
     conv0_ref, h0_ref) = refs[:N_MIXER_A_INPUTS]
    cast_in = refs[N_MIXER_A_INPUTS:N_MIXER_A_INPUTS + n_cast]
    mixq_ref, convs_ref, lrus_ref = refs[N_MIXER_A_INPUTS + n_cast:N_MIXER_A_INPUTS + n_cast + 3]
    cast_out = refs[N_MIXER_A_INPUTS + n_cast + 3:N_MIXER_A_INPUTS + 2 * n_cast + 3]
    xpad_s, h_s, carry_s = refs[N_MIXER_A_INPUTS + 2 * n_cast + 3:]
    for src_ref, dst_ref in zip(cast_in, cast_out):
        dst_ref[...] = src_ref[...].astype(BF16)
    t_idx = pl.program_id(1)
    rows = bb * tt
    tsub = MXU_DIM // bb
    hist = (CONV_WIDTH - 1) * bb

    @pl.when(t_idx == 0)
    def _init():
        xpad_s[0:hist, :] = conv0_ref[...].reshape(hist, LRU_WIDTH)
        carry_s[...] = h0_ref[...]

    sp_neg_lam = _softplus(-lam_ref[...])

    def project(s):
        xs = x_ref[:, s * tsub:(s + 1) * tsub, :].reshape(MXU_DIM, D_MODEL)
        hn = _rmsnorm(xs, g_ref[...]).astype(BF16)
        return _dot(_dot(perm_ref[...], hn).astype(BF16), win_ref[...])

    def conv_and_gates(s, proj):
        lo = s * MXU_DIM
        xpad_s[hist + lo:hist + lo + MXU_DIM, :] = proj[:, :LRU_WIDTH]
        xc = cb_ref[...] + xpad_s[hist + lo:hist + lo + MXU_DIM, :] * cw_ref[CONV_WIDTH - 1:CONV_WIDTH, :]
        for k in range(CONV_WIDTH - 1):
            xc = xc + xpad_s[k * bb + lo:k * bb + lo + MXU_DIM, :] * cw_ref[k:k + 1, :]
        xcb = xc.astype(BF16)
        r_parts, i_parts = [], []
        for gi in range(N_GATE_GROUPS):
            sl = slice(gi * GATE_GROUP, (gi + 1) * GATE_GROUP)
            r_parts.append(_dot(xcb[:, sl], wa_ref[gi]))
            i_parts.append(_dot(xcb[:, sl], wi_ref[gi]))
        return xc, jnp.concatenate(r_parts, axis=-1), jnp.concatenate(i_parts, axis=-1)

    def recur_and_emit(s, proj, xc, r_pre, i_pre, h):
        r = jax.nn.sigmoid(r_pre + ba_ref[...])
        ig = jax.nn.sigmoid(i_pre + bi_ref[...])
        log_a = (-LRU_C * r) * sp_neg_lam
        a = jnp.exp(log_a)
        th = jnp.tanh(log_a)
        z = -2.0 * th / (1.0 - th)
        mult = jnp.where(z > 0.0, z * lax.rsqrt(z), 0.0)
        if reset_first and s == 0:
            row = lax.broadcasted_iota(jnp.int32, (MXU_DIM, 1), 0)
            mult = jnp.where(jnp.logical_and(row < bb, t_idx == 0), 1.0, mult)
        u = mult * (ig * xc)
        lo = s * MXU_DIM
        for t in range(tsub):
            h = a[t * bb:(t + 1) * bb, :] * h + u[t * bb:(t + 1) * bb, :]
            h_s[lo + t * bb:lo + (t + 1) * bb, :] = h
        mix = (h_s[lo:lo + MXU_DIM, :] * _gelu_tanh(proj[:, LRU_WIDTH:2 * LRU_WIDTH])).astype(BF16)
        mixq_tm = jnp.concatenate([mix, (proj[:, 2 * LRU_WIDTH:] * Q_SCALE).astype(BF16)], axis=-1)
        blk = _dot(perm_t_ref[...], mixq_tm).astype(BF16)
        mixq_ref[:, s * tsub:(s + 1) * tsub, :] = blk.reshape(bb, tsub, MIX_WIDTH)
        return h

    n_sub = tt // tsub
    h = carry_s[...]
    proj_next = project(0)
    for s in range(n_sub):
        proj = proj_next
        if s + 1 < n_sub:
            proj_next = project(s + 1)
        xc, r_pre, i_pre = conv_and_gates(s, proj)
        h = recur_and_emit(s, proj, xc, r_pre, i_pre, h)
    carry_s[...] = h
    lrus_ref[...] = h
    tail = xpad_s[rows:rows + hist, :]
    convs_ref[...] = tail.reshape(CONV_WIDTH - 1, bb, LRU_WIDTH)
    xpad_s[0:hist, :] = tail


def _mixer_a(x, g, w_in, conv_w, conv_b, wa_bd, wi_bd, b_a, b_i, lam, conv0_tm, h0, *, bb, tt,
             reset_first, cast_weights=()):
    b, t, _ = x.shape
    assert bb == SUBLANES and b % bb == 0 and t % tt == 0 and MXU_DIM % bb == 0
    tsub = MXU_DIM // bb
    assert tt % tsub == 0
    rows = bb * tt
    n_steps = (b // bb) * (t // tt)
    nt = t // tt
    cast_specs = []
    for w in cast_weights:
        blk_rows = w.shape[-2] // n_steps
        assert w.shape[-2] % n_steps == 0 and blk_rows % 16 == 0
        lead = w.shape[:-2]
        cast_specs.append(pl.BlockSpec(
            lead + (blk_rows, w.shape[-1]),
            lambda i, j, _n=len(lead): (0,) * _n + (i * nt + j, 0)))
    tm = jnp.arange(MXU_DIM)
    perm = (((tm % bb) * tsub + tm // bb)[:, None] == jnp.arange(MXU_DIM)[None, :]).astype(BF16)
    kern = functools.partial(_mixer_a_kernel, bb=bb, tt=tt, reset_first=reset_first,
                             n_cast=len(cast_weights))
    return pl.pallas_call(
        kern,
        grid=(b // bb, t // tt),
        in_specs=[
            pl.BlockSpec((bb, tt, D_MODEL), lambda i, j: (i, j, 0)),
            _const_spec((MXU_DIM, MXU_DIM)),
            _const_spec((MXU_DIM, MXU_DIM)),
            _const_spec((1, D_MODEL)),
            _const_spec((D_MODEL, 2 * LRU_WIDTH + MEM_WIDTH)),
            _const_spec((CONV_WIDTH, LRU_WIDTH)),
            _const_spec((1, LRU_WIDTH)),
            _const_spec((N_GATE_GROUPS, GATE_GROUP, GATE_GROUP)),
            _const_spec((N_GATE_GROUPS, GATE_GROUP, GATE_GROUP)),
            _const_spec((1, LRU_WIDTH)),
            _const_spec((1, LRU_WIDTH)),
            _const_spec((1, LRU_WIDTH)),
            pl.BlockSpec((CONV_WIDTH - 1, bb, LRU_WIDTH), lambda i, j: (0, i, 0)),
            pl.BlockSpec((bb, LRU_WIDTH), lambda i, j: (i, 0)),
        ] + cast_specs,
        out_specs=[
            pl.BlockSpec((bb, tt, MIX_WIDTH), lambda i, j: (i, j, 0)),
            pl.BlockSpec((CONV_WIDTH - 1, bb, LRU_WIDTH), lambda i, j: (0, i, 0)),
            pl.BlockSpec((bb, LRU_WIDTH), lambda i, j: (i, 0)),
        ] + cast_specs,
        out_shape=[
            jax.ShapeDtypeStruct((b, t, MIX_WIDTH), BF16),
            jax.ShapeDtypeStruct((CONV_WIDTH - 1, b, LRU_WIDTH), F32),
            jax.ShapeDtypeStruct((b, LRU_WIDTH), F32),
        ] + [jax.ShapeDtypeStruct(w.shape, BF16) for w in cast_weights],
        scratch_shapes=[
            pltpu.VMEM(((CONV_WIDTH - 1) * bb + rows, LRU_WIDTH), F32),
            pltpu.VMEM((rows, LRU_WIDTH), F32),
            pltpu.VMEM((bb, LRU_WIDTH), F32),
        ],
        compiler_params=_params(2),
        name="mixer_a",
    )(x, perm, perm.T, g, w_in, conv_w, conv_b, wa_bd, wi_bd, b_a, b_i, lam, conv0_tm, h0, *cast_weights)


def _post_kernel(x_ref, mixq_ref, mkt_ref, mvt_ref, wo_ref, gf_ref, wgu_ref, wd_ref, gt_ref, *rest,
                 bb, tq, tail):
    if tail == 'final':
        y_ref, cat_s = rest
    else:
        wkv_ref, x2_ref, k_out_ref, v_out_ref, k_att_ref, v_att_ref, cat_s = rest
    rows = bb * tq
    mix_w = MIX_WIDTH - MEM_WIDTH

    cat_s[:, :mix_w] = mixq_ref[:, :, :mix_w].reshape(rows, mix_w)
    for b in range(bb):
        q = mixq_ref[b, :, mix_w:].astype(F32)
        s = _group_scores(_stack_heads(q), mkt_ref[b].astype(BF16), k_is_transposed=True)
        e, row_sum = _group_exp(s)
        o = _group_values(e, row_sum, mvt_ref[b].astype(BF16), v_is_transposed=True)
        cat_s[b * tq:(b + 1) * tq, mix_w:] = o.astype(BF16)

    x = x_ref[...].reshape(rows, D_MODEL)
    x1 = x + _dot(cat_s[...], wo_ref[...])
    hf = _rmsnorm(x1, gf_ref[...]).astype(BF16)
    acts = []
    for c0, c1 in FF_CHUNKS:
        gg = _dot(hf, wgu_ref[:, c0:c1])
        uu = _dot(hf, wgu_ref[:, D_FF + c0:D_FF + c1])
        acts.append(((gg * jax.nn.sigmoid(gg)) * uu).astype(BF16))
    x2 = x1 + _dot(jnp.concatenate(acts, axis=-1), wd_ref[...])

    if tail == 'final':
        y_ref[...] = _rmsnorm(x2, gt_ref[...]).reshape(bb, tq, D_MODEL)
        return
    x2_ref[...] = x2.reshape(bb, tq, D_MODEL)
    hk = _rmsnorm(x2, gt_ref[...]).astype(BF16)
    kv = _dot(hk, wkv_ref[...])
    k = kv[:, :ATT_WIDTH]
    v = kv[:, ATT_WIDTH:]
    if tail == 'kv_prompt':
        k_att_ref[...] = k.reshape(bb, tq, ATT_WIDTH).astype(BF16)
        v_att_ref[...] = v.reshape(bb, tq, ATT_WIDTH).astype(BF16)

        @pl.when(pl.program_id(1) == pl.num_programs(1) - 1)
        def _write_cache():
            k_out_ref[0] = k.T
            v_out_ref[0] = v.T
    else:
        for b in range(bb):
            for h in range(N_ATT_HEADS):
                hs = slice(h * HEAD_DIM, (h + 1) * HEAD_DIM)
                k_out_ref[b, h] = k[b * tq:(b + 1) * tq, hs]
                v_out_ref[b, h] = v[b * tq:(b + 1) * tq, hs]
        k_att_ref[...] = k.T.astype(BF16)
        v_att_ref[...] = v.T.astype(BF16)


def _post(x, mixq, mem_kt, mem_vt, layer, w_out, g_ffn, w_gu, w_d, g_tail, w_kv=None, *, bb, tq, tail):
    b, t, _ = x.shape
    assert b % bb == 0 and t % tq == 0
    tile = lambda w: pl.BlockSpec((bb, tq, w), lambda i, j: (i, j, 0))
    mem_spec = pl.BlockSpec((None, bb, MEM_WIDTH, N_MEM), lambda i, j: (layer, i, 0, 0))
    in_specs = [
        tile(D_MODEL), tile(MIX_WIDTH), mem_spec, mem_spec,
        _layer_spec((MIX_WIDTH, D_MODEL), layer),
        _layer_spec((1, D_MODEL), layer),
        _layer_spec((D_MODEL, 2 * D_FF), layer),
        _layer_spec((D_FF, D_MODEL), layer),
        _const_spec((1, D_MODEL)),
    ]
    args = [x, mixq, mem_kt, mem_vt, w_out, g_ffn, w_gu, w_d, g_tail]
    if tail == 'final':
        out_specs = [tile(D_MODEL)]
        out_shape = [jax.ShapeDtypeStruct((b, t, D_MODEL), F32)]
    else:
        in_specs.append(_const_spec((D_MODEL, 2 * ATT_WIDTH)))
        args.append(w_kv)
        if tail == 'kv_prompt':
            assert bb == 1 and tq == BAND_ROWS
            cache_spec = pl.BlockSpec((1, ATT_WIDTH, BAND_ROWS), lambda i, j: (i, 0, 0))
            out_specs = [tile(D_MODEL), cache_spec, cache_spec, tile(ATT_WIDTH), tile(ATT_WIDTH)]
            out_shape = [jax.ShapeDtypeStruct((b, t, D_MODEL), F32),
                         jax.ShapeDtypeStruct((b, ATT_WIDTH, BAND_ROWS), F32),
                         jax.ShapeDtypeStruct((b, ATT_WIDTH, BAND_ROWS), F32),
                         jax.ShapeDtypeStruct((b, t, ATT_WIDTH), BF16),
                         jax.ShapeDtypeStruct((b, t, ATT_WIDTH), BF16)]
        else:
            assert tq == t
            head_spec = pl.BlockSpec((bb, N_ATT_HEADS, t, HEAD_DIM), lambda i, j: (i, 0, 0, 0),
                                     pipeline_mode=pl.Buffered(1))
            kt_spec = pl.BlockSpec((None, ATT_WIDTH, bb * t), lambda i, j: (i, 0, 0))
            out_specs = [tile(D_MODEL), head_spec, head_spec, kt_spec, kt_spec]
            out_shape = [jax.ShapeDtypeStruct((b, t, D_MODEL), F32),
                         jax.ShapeDtypeStruct((b, N_ATT_HEADS, t, HEAD_DIM), F32),
                         jax.ShapeDtypeStruct((b, N_ATT_HEADS, t, HEAD_DIM), F32),
                         jax.ShapeDtypeStruct((b // bb, ATT_WIDTH, bb * t), BF16),
                         jax.ShapeDtypeStruct((b // bb, ATT_WIDTH, bb * t), BF16)]
    kern = functools.partial(_post_kernel, bb=bb, tq=tq, tail=tail)
    return pl.pallas_call(
        kern,
        grid=(b // bb, t // tq),
        in_specs=in_specs,
        out_specs=out_specs,
        out_shape=out_shape,
        scratch_shapes=[pltpu.VMEM((bb * tq, MIX_WIDTH), BF16)],
        compiler_params=_params(2),
        name="post_" + tail,
    )(*args)


def _build_band_bias(tab_ref, bias_s, tq):
    kw = BAND_ROWS + tq
    r_io = lax.broadcasted_iota(jnp.int32, (tq, kw), 0)
    i_io = lax.broadcasted_iota(jnp.int32, (tq, kw), 1)
    off = i_io - jnp.bitwise_and(r_io, -CHUNK)
    in_band = jnp.logical_and(off >= 0, off < BAND_KEYS)
    for h in range(N_ATT_HEADS):
        row = jnp.broadcast_to(tab_ref[h:h + 1, :], (tq, BIAS_TABLE_LEN))
        rolled = pltpu.roll(row, 0, 1, stride=1, stride_axis=0)[:, :kw]
        bias_s[h * tq:(h + 1) * tq, :] = jnp.where(in_band, rolled * LOG2_E, NEG_INF)


def _project_queries(x_ref, g_ref, win_ref, mixq_ref, qst_s, bb, tq, ts):
    rows = bb * tq
    x = x_ref[...].reshape(rows, D_MODEL)
    hn = _rmsnorm(x, g_ref[...]).astype(BF16)
    proj = _dot(hn, win_ref[...])
    mixq_ref[:, :, ATT_WIDTH:] = (proj[:, ATT_WIDTH:] * Q_SCALE).reshape(bb, tq, MEM_WIDTH).astype(BF16)
    lane_head = _lane_head(rows)
    for g in range(N_ATT_GROUPS):
        qg = proj[:, g * MXU_DIM:(g + 1) * MXU_DIM] * Q_SCALE
        for h in range(HEADS_PER_GROUP):
            blk = jnp.where(lane_head == h, qg, 0.0).astype(BF16)
            qst_s[:, g, h * ts:(h + 1) * ts, :] = blk.reshape(rows // ts, ts, MXU_DIM)


def _mixer_b_sample_kernel(x_ref, g_ref, win_ref, ktn_ref, vtn_ref, pkt_ref, pvt_ref, tab_ref,
                           mixq_ref, kt_s, vt_s, bias_s, qst_s, *, bb, tq):
    group_rows = HEADS_PER_GROUP * tq

    @pl.when(pl.program_id(0) == 0)
    def _build_bias():
        _build_band_bias(tab_ref, bias_s, tq)

    kt_s[:, :, :BAND_ROWS] = pkt_ref[...].astype(BF16)
    vt_s[:, :, :BAND_ROWS] = pvt_ref[...].astype(BF16)
    for b in range(bb):
        kt_s[b, :, BAND_ROWS:] = ktn_ref[:, b * tq:(b + 1) * tq]
        vt_s[b, :, BAND_ROWS:] = vtn_ref[:, b * tq:(b + 1) * tq]
    _project_queries(x_ref, g_ref, win_ref, mixq_ref, qst_s, bb, tq, tq)

    for b in range(bb):
        for g in range(N_ATT_GROUPS):
            ls = slice(g * MXU_DIM, (g + 1) * MXU_DIM)
            s = _group_scores(qst_s[b, g], kt_s[b, ls, :], bias_s[g * group_rows:(g + 1) * group_rows, :],
                              k_is_transposed=True)
            e, row_sum = _group_exp(s)
            out = _group_values(e, row_sum, vt_s[b, ls, :], v_is_transposed=True)
            mixq_ref[b, :, ls] = out.astype(BF16)


def _mixer_b_sample(x, g, w_in, kt_new, vt_new, past_kt, past_vt, table, *, bb):
    b, t, _ = x.shape
    tq = t
    nb = kt_new.shape[2] // t
    assert t == CHUNK and b % bb == 0 and nb % bb == 0 and (bb * t) % 128 == 0
    per = nb // bb
    kw = BAND_ROWS + tq
    new_spec = pl.BlockSpec((None, ATT_WIDTH, bb * t), lambda i: (i // per, 0, i % per))
    past_spec = pl.BlockSpec((bb, ATT_WIDTH, BAND_ROWS), lambda i: (i, 0, 0))
    return pl.pallas_call(
        functools.partial(_mixer_b_sample_kernel, bb=bb, tq=tq),
        grid=(b // bb,),
        in_specs=[
            pl.BlockSpec((bb, tq, D_MODEL), lambda i: (i, 0, 0)),
            _const_spec((1, D_MODEL)),
            _const_spec((D_MODEL, ATT_WIDTH + MEM_WIDTH)),
            new_spec, new_spec, past_spec, past_spec,
            _const_spec((N_ATT_HEADS, BIAS_TABLE_LEN)),
        ],
        out_specs=pl.BlockSpec((bb, tq, MIX_WIDTH), lambda i: (i, 0, 0)),
        out_shape=jax.ShapeDtypeStruct((b, t, MIX_WIDTH), BF16),
        scratch_shapes=[
            pltpu.VMEM((bb, ATT_WIDTH, kw), BF16),
            pltpu.VMEM((bb, ATT_WIDTH, kw), BF16),
            pltpu.VMEM((N_ATT_HEADS * tq, kw), F32),
            pltpu.VMEM((bb, N_ATT_GROUPS, HEADS_PER_GROUP * tq, MXU_DIM), BF16),
        ],
        compiler_params=_params(1),
        name="mixer_b_sample",
    )(x, g, w_in, kt_new, vt_new, past_kt, past_vt, table)


def _mixer_b_kernel(x_ref, g_ref, win_ref, kb_ref, vb_ref, tab_ref, mixq_ref, kpad_s, vpad_s, bias_s,
                    qst_s, *, tq, ts):
    bi = pl.program_id(0)
    j = pl.program_id(1)
    kw = BAND_ROWS + ts
    group_rows = HEADS_PER_GROUP * ts

    @pl.when(jnp.logical_and(bi == 0, j == 0))
    def _build_bias():
        _build_band_bias(tab_ref, bias_s, ts)

    @pl.when(j == 0)
    def _fill_kv():
        kpad_s[:, :BAND_ROWS, :] = jnp.zeros((1, BAND_ROWS, ATT_WIDTH), BF16)
        vpad_s[:, :BAND_ROWS, :] = jnp.zeros((1, BAND_ROWS, ATT_WIDTH), BF16)
        kpad_s[:, BAND_ROWS:, :] = kb_ref[...]
        vpad_s[:, BAND_ROWS:, :] = vb_ref[...]

    _project_queries(x_ref, g_ref, win_ref, mixq_ref, qst_s, 1, tq, ts)
    lanes = [slice(g * MXU_DIM, (g + 1) * MXU_DIM) for g in range(N_ATT_GROUPS)]

    def attend(mask_keys):
        units = [(u, g) for u in range(tq // ts) for g in range(N_ATT_GROUPS)]

        def scores(u, g):
            start = pl.multiple_of(j * tq + u * ts, ts)
            key_ok = None
            if mask_keys:
                key_ok = lax.broadcasted_iota(jnp.int32, (1, kw), 1) >= BAND_ROWS - start
            return _group_scores(qst_s[u, g], kpad_s[0, pl.ds(start, kw), lanes[g]],
                                 bias_s[g * group_rows:(g + 1) * group_rows, :], key_ok)

        s_next = scores(*units[0])
        for n, (u, g) in enumerate(units):
            s_cur = s_next
            if n + 1 < len(units):
                s_next = scores(*units[n + 1])
            e, row_sum = _group_exp(s_cur)
            start = pl.multiple_of(j * tq + u * ts, ts)
            out = _group_values(e, row_sum, vpad_s[0, pl.ds(start, kw), lanes[g]])
            mixq_ref[0, u * ts:(u + 1) * ts, lanes[g]] = out.astype(BF16)

    n_masked_tiles = BAND_ROWS // tq
    pl.when(j < n_masked_tiles)(lambda: attend(True))
    pl.when(j >= n_masked_tiles)(lambda: attend(False))


def _mixer_b(x, g, w_in, kb, vb, table, *, tq, ts):
    b, t, _ = x.shape
    assert t % tq == 0 and tq % ts == 0 and ts % CHUNK == 0 and BAND_ROWS % tq == 0
    assert BAND_ROWS + 2 * ts - 1 <= BIAS_TABLE_LEN
    kw = BAND_ROWS + ts
    return pl.pallas_call(
        functools.partial(_mixer_b_kernel, tq=tq, ts=ts),
        grid=(b, t // tq),
        in_specs=[
            pl.BlockSpec((1, tq, D_MODEL), lambda i, j: (i, j, 0)),
            _const_spec((1, D_MODEL)),
            _const_spec((D_MODEL, ATT_WIDTH + MEM_WIDTH)),
            pl.BlockSpec((1, t, ATT_WIDTH), lambda i, j: (i, 0, 0)),
            pl.BlockSpec((1, t, ATT_WIDTH), lambda i, j: (i, 0, 0)),
            _const_spec((N_ATT_HEADS, BIAS_TABLE_LEN)),
        ],
        out_specs=pl.BlockSpec((1, tq, MIX_WIDTH), lambda i, j: (i, j, 0)),
        out_shape=jax.ShapeDtypeStruct((b, t, MIX_WIDTH), BF16),
        scratch_shapes=[
            pltpu.VMEM((1, BAND_ROWS + t, ATT_WIDTH), BF16),
            pltpu.VMEM((1, BAND_ROWS + t, ATT_WIDTH), BF16),
            pltpu.VMEM((N_ATT_HEADS * ts, kw), F32),
            pltpu.VMEM((tq // ts, N_ATT_GROUPS, HEADS_PER_GROUP * ts, MXU_DIM), BF16),
        ],
        compiler_params=_params(2),
        name="mixer_b",
    )(x, g, w_in, kb, vb, table)


def _block_diag_groups(w):
    per = GATE_GROUP // HEAD_DIM
    w4 = w.reshape(N_GATE_GROUPS, per, HEAD_DIM, HEAD_DIM)
    eye = jnp.eye(per, dtype=w.dtype)
    bd = w4[:, :, :, None, :] * eye[None, :, None, :, None]
    return bd.reshape(N_GATE_GROUPS, GATE_GROUP, GATE_GROUP)


def _bias_table(rel):
    top = rel[-1:]
    head = jnp.broadcast_to(top, (BAND_ROWS - REL_CLIP, rel.shape[1]))
    tail = jnp.broadcast_to(top, (BIAS_TABLE_LEN - (BAND_ROWS - REL_CLIP) - rel.shape[0], rel.shape[1]))
    return jnp.concatenate([head, rel[::-1], tail], axis=0).T


def _trunk(x, is_prompt, conv0, h0, past_kt, past_vt, mem_kt, mem_vt, p, *, a_tile, post_tile, b_tile):
    b, t, _ = x.shape
    cast_names = tuple(p.get('f32_weights', {}))
    mixq, conv_s, lru_s, *cast = _mixer_a(
        x, p['g_mix'][0], p['w_in_a'], p['conv_w'], p['conv_b'], p['wa_bd'], p['wi_bd'], p['b_rg_a'],
        p['b_rg_i'], p['lam'], jnp.swapaxes(conv0, 0, 1), h0, bb=a_tile[0], tt=a_tile[1],
        reset_first=is_prompt, cast_weights=tuple(p['f32_weights'][n] for n in cast_names))
    if cast_names:
        p.update(zip(cast_names, cast))
        del p['f32_weights']
    x, k_out, v_out, k_att, v_att = _post(
        x, mixq, mem_kt, mem_vt, 0, p['w_out'], p['g_ffn'], p['w_gu'], p['w_d'], p['g_kv'], p['w_kv'],
        bb=post_tile[0], tq=post_tile[1], tail='kv_prompt' if is_prompt else 'kv_sample')
    if is_prompt:
        mixq = _mixer_b(x, p['g_mix'][1], p['w_in_b'], k_att, v_att, p['bias_table'], tq=b_tile[0],
                        ts=b_tile[1])
        k_new = jnp.transpose(k_out.reshape(b, N_ATT_HEADS, HEAD_DIM, -1), (0, 3, 1, 2))
        v_new = jnp.transpose(v_out.reshape(b, N_ATT_HEADS, HEAD_DIM, -1), (0, 3, 1, 2))
    else:
        mixq = _mixer_b_sample(x, p['g_mix'][1], p['w_in_b'], k_att, v_att, past_kt, past_vt,
                               p['bias_table'], bb=b_tile[0])
        k_new = jnp.transpose(k_out, (0, 2, 1, 3))
        v_new = jnp.transpose(v_out, (0, 2, 1, 3))
    (y,) = _post(
        x, mixq, mem_kt, mem_vt, 1, p['w_out'], p['g_ffn'], p['w_gu'], p['w_d'], p['g_final'],
        bb=post_tile[0], tq=post_tile[1], tail='final')
    return y, jnp.swapaxes(conv_s, 0, 1)[None], lru_s[None], k_new, v_new


def kernel(x_prompt, x_sample, state_conv, state_lru, cache_k, cache_v, cache_mem_k, cache_mem_v, mem_prompt, g_mix, g_ffn, g_final, w_in_a, conv_w, conv_b, w_rg_a, b_rg_a, w_rg_i, b_rg_i, lru_lambda, g_kv, w_kv, w_in_b, rel_bias, g_mem, w_mem_kv, w_out, w_ffn_gu, w_ffn_down):
    depth = g_mix.shape[0]
    row = lambda v: v.reshape(1, -1)
    p = {
        'g_mix': [row(g_mix[l]) for l in range(depth)],
        'g_ffn': g_ffn.reshape(depth, 1, D_MODEL),
        'g_final': row(g_final), 'g_kv': row(g_kv),
        'w_in_a': w_in_a[0].astype(BF16), 'conv_w': conv_w[0], 'conv_b': row(conv_b[0]),
        'wa_bd': _block_diag_groups(w_rg_a[0]).astype(BF16),
        'wi_bd': _block_diag_groups(w_rg_i[0]).astype(BF16),
        'b_rg_a': row(b_rg_a[0]), 'b_rg_i': row(b_rg_i[0]), 'lam': row(lru_lambda[0]),
        'w_in_b': w_in_b[0].astype(BF16), 'bias_table': _bias_table(rel_bias[0]),
        'f32_weights': {'w_kv': w_kv, 'w_out': w_out, 'w_gu': w_ffn_gu, 'w_d': w_ffn_down},
    }
    bp = x_prompt.shape[0]
    bs = x_sample.shape[0]

    def keys_minor(a):
        nd = a.ndim
        a = jnp.transpose(a, tuple(range(nd - 3)) + (nd - 2, nd - 1, nd - 3))
        return a.reshape(a.shape[:nd - 3] + (a.shape[nd - 3] * a.shape[nd - 2], a.shape[nd - 1]))

    def keys_major(a, heads):
        nd = a.ndim
        a = a.reshape(a.shape[:nd - 2] + (heads, HEAD_DIM, a.shape[nd - 1]))
        return jnp.transpose(a, tuple(range(nd - 2)) + (nd, nd - 2, nd - 1))

    mem_kt_p, mem_vt_p = _mem_kv(mem_prompt, g_mem, w_mem_kv.astype(BF16))
    conv0 = jnp.zeros((bp, CONV_WIDTH - 1, LRU_WIDTH), F32)
    lru0 = jnp.zeros((bp, LRU_WIDTH), F32)
    y_p, conv_p, lru_p, k_p, v_p = _trunk(
        x_prompt, True, conv0, lru0, None, None, mem_kt_p, mem_vt_p, p,
        a_tile=(8, 128), post_tile=(1, 512), b_tile=(512, 64))

    y_s, conv_s, lru_s, k_s, v_s = _trunk(
        x_sample, False, state_conv[0], state_lru[0], keys_minor(cache_k), keys_minor(cache_v),
        keys_minor(cache_mem_k), keys_minor(cache_mem_v), p,
        a_tile=(8, 64), post_tile=(8, 64), b_tile=(4, 64))

    return (y_p, y_s, conv_p, lru_p, k_p, v_p, keys_major(mem_kt_p, N_MEM_HEADS),
            keys_major(mem_vt_p, N_MEM_HEADS), conv_s, lru_s, k_s, v_s)
```
